```python
import jax, jax.numpy as jnp
from jax import lax
import numpy as np

D_MODEL = 2048
BATCH = 4
SEQ = 8192
DEPTH = 1

D_MIX = D_MODEL
D_RET = D_MIX // 2
D_SGU = D_MIX - D_RET
RET_HEADS = 8
RET_HEAD_DIM = D_RET // RET_HEADS
SGU_GROUPS = 8
SGU_GROUP_DIM = D_SGU // SGU_GROUPS
CHUNK = 128
D_FF = 5632
ROPE_BASE = 10000.0
EPS = 1e-6
D_PROJ = 4 * D_RET + 2 * D_SGU

kernel_name = "hybrid_retention_sgu_macaron_layer"


def rmsnorm(x, g):
    xf = x.astype(jnp.float32)
    y = xf * lax.rsqrt(jnp.mean(xf * xf, axis=-1, keepdims=True) + EPS)
    return (y * g.astype(jnp.float32)).astype(x.dtype)


def layernorm_nobias(x, g):
    xf = x.astype(jnp.float32)
    mu = jnp.mean(xf, axis=-1, keepdims=True)
    xc = xf - mu
    y = xc * lax.rsqrt(jnp.mean(xc * xc, axis=-1, keepdims=True) + EPS)
    return (y * g.astype(jnp.float32)).astype(x.dtype)


def swiglu(x, w_gate, w_up, w_down):
    return (jax.nn.silu(x @ w_gate) * (x @ w_up)) @ w_down


def rotary(x, positions):
    half = x.shape[-1] // 2
    inv_freq = ROPE_BASE ** (-jnp.arange(half, dtype=jnp.float32) / half)
    ang = positions.astype(jnp.float32)[..., None] * inv_freq
    cos = jnp.cos(ang)[:, :, None, :]
    sin = jnp.sin(ang)[:, :, None, :]
    xf = x.astype(jnp.float32)
    x1, x2 = xf[..., :half], xf[..., half:]
    return jnp.concatenate([x1 * cos - x2 * sin, x1 * sin + x2 * cos], axis=-1)


def retention_chunkwise(q, k, v):
    b, s, h, d = q.shape
    n = s // CHUNK
    log_gamma = jnp.log1p(-jnp.exp2(-5.0 - jnp.arange(h, dtype=jnp.float32)))
    idx = jnp.arange(CHUNK, dtype=jnp.float32)
    rel = idx[:, None] - idx[None, :]
    causal = rel >= 0
    decay_in = jnp.where(causal[None], jnp.exp(log_gamma[:, None, None] * jnp.where(causal, rel, 0.0)[None]), 0.0)
    xi = jnp.exp(log_gamma[None, :] * (idx + 1.0)[:, None])
    zeta = jnp.exp(log_gamma[None, :] * (CHUNK - 1.0 - idx)[:, None])
    chunk_decay = jnp.exp(log_gamma * CHUNK)

    qc = q.reshape(b, n, CHUNK, h, d)
    kc = k.reshape(b, n, CHUNK, h, d)
    vc = v.reshape(b, n, CHUNK, h, d)

    scores = jnp.einsum('bnchd,bnmhd->bnhcm', qc, kc) * decay_in[None, None]
    inner = jnp.einsum('bnhcm,bnmhe->bnche', scores, vc)

    kv = jnp.einsum('bnmhd,bnmhe->bnhde', kc * zeta[None, None, :, :, None], vc)

    def step(state, kv_n):
        return state * chunk_decay[None, :, None, None] + kv_n, state

    _, prev = lax.scan(step, jnp.zeros((b, h, d, d), jnp.float32), jnp.moveaxis(kv, 1, 0))
    prev = jnp.moveaxis(prev, 0, 1)
    cross = jnp.einsum('bnchd,bnhde->bnche', qc * xi[None, None, :, :, None], prev)
    return (inner + cross).reshape(b, s, h, d)


def spatial_gating(u, v, norm_g, w_s, b_s):
    b, s, _ = u.shape
    n = s // CHUNK
    v = layernorm_nobias(v, norm_g)
    vc = v.reshape(b, n, CHUNK, SGU_GROUPS, SGU_GROUP_DIM)
    mask = jnp.tril(jnp.ones((CHUNK, CHUNK), dtype=bool))
    w = jnp.where(mask[None], w_s, jnp.zeros((), w_s.dtype))
    mixed = jnp.einsum('gts,bnsgc->bntgc', w, vc) + b_s.T[None, None, :, :, None]
    return u * mixed.reshape(b, s, D_SGU)


def setup_inputs(seed: int = 0) -> dict:
    key = jax.random.key(seed)
    ks = jax.random.split(key, 20)
    f32 = jnp.float32

    def normal(k, shape, scale):
        return jax.random.normal(k, shape, f32) * scale

    def gain(k, dim):
        return 1.0 + 0.05 * jax.random.normal(k, (DEPTH, dim), f32)

    x = jax.random.normal(ks[0], (BATCH, SEQ, D_MODEL), f32)
    positions = jnp.broadcast_to(jnp.arange(SEQ, dtype=jnp.int32)[None, :], (BATCH, SEQ))
    return {
        "x": x,
        "positions": positions,
        "ffn1_pre_g": gain(ks[1], D_MODEL),
        "ffn1_w_gate": normal(ks[2], (DEPTH, D_MODEL, D_FF), D_MODEL ** -0.5),
        "ffn1_w_up": normal(ks[3], (DEPTH, D_MODEL, D_FF), D_MODEL ** -0.5),
        "ffn1_w_down": normal(ks[4], (DEPTH, D_FF, D_MODEL), D_FF ** -0.5),
        "ffn1_post_g": gain(ks[5], D_MODEL),
        "mix_pre_g": gain(ks[6], D_MODEL),
        "w_in": normal(ks[7], (DEPTH, D_MODEL, D_PROJ), D_MODEL ** -0.5),
        "ret_norm_g": gain(ks[8], D_RET),
        "sgu_norm_g": gain(ks[9], D_SGU),
        "sgu_w_s": normal(ks[10], (DEPTH, SGU_GROUPS, CHUNK, CHUNK), 0.05),
        "sgu_b_s": 1.0 + 0.1 * jax.random.normal(ks[11], (DEPTH, SGU_GROUPS, CHUNK), f32),
        "w_out": normal(ks[12], (DEPTH, D_MIX, D_MODEL), D_MIX ** -0.5),
        "mix_post_g": gain(ks[13], D_MODEL),
        "ffn2_pre_g": gain(ks[14], D_MODEL),
        "ffn2_w_gate": normal(ks[15], (DEPTH, D_MODEL, D_FF), D_MODEL ** -0.5),
        "ffn2_w_up": normal(ks[16], (DEPTH, D_MODEL, D_FF), D_MODEL ** -0.5),
        "ffn2_w_down": normal(ks[17], (DEPTH, D_FF, D_MODEL), D_FF ** -0.5),
        "ffn2_post_g": gain(ks[18], D_MODEL),
    }


def reference(x, positions, ffn1_pre_g, ffn1_w_gate, ffn1_w_up, ffn1_w_down, ffn1_post_g,
              mix_pre_g, w_in, ret_norm_g, sgu_norm_g, sgu_w_s, sgu_b_s, w_out, mix_post_g,
              ffn2_pre_g, ffn2_w_gate, ffn2_w_up, ffn2_w_down, ffn2_post_g):
    b, s, _ = x.shape
    split_at = [D_RET, 2 * D_RET, 3 * D_RET, 4 * D_RET, 4 * D_RET + D_SGU]
    for l in range(DEPTH):
        h = swiglu(rmsnorm(x, ffn1_pre_g[l]), ffn1_w_gate[l], ffn1_w_up[l], ffn1_w_down[l])
        x = x + 0.5 * rmsnorm(h, ffn1_post_g[l])

        h = rmsnorm(x, mix_pre_g[l])
        z = h @ w_in[l]
        q, k, v, g, u, vs = jnp.split(z, split_at, axis=-1)

        q = q.reshape(b, s, RET_HEADS, RET_HEAD_DIM)
        k = k.reshape(b, s, RET_HEADS, RET_HEAD_DIM)
        v = v.reshape(b, s, RET_HEADS, RET_HEAD_DIM).astype(jnp.float32)
        q = rotary(q, positions) * (RET_HEAD_DIM ** -0.5)
        k = rotary(k, positions)
        ret = retention_chunkwise(q, k, v)
        ret = layernorm_nobias(ret, ret_norm_g[l].reshape(RET_HEADS, RET_HEAD_DIM))
        ret = ret.reshape(b, s, D_RET).astype(x.dtype) * jax.nn.silu(g)

        sgu = spatial_gating(jax.nn.gelu(u), jax.nn.gelu(vs), sgu_norm_g[l], sgu_w_s[l], sgu_b_s[l])

        mix = jnp.concatenate([ret, sgu], axis=-1) @ w_out[l]
        x = x + rmsnorm(mix, mix_post_g[l])

        h = swiglu(rmsnorm(x, ffn2_pre_g[l]), ffn2_w_gate[l], ffn2_w_up[l], ffn2_w_down[l])
        x = x + 0.5 * rmsnorm(h, ffn2_post_g[l])
    return x
```

```python
import functools
import math

import jax
import jax.numpy as jnp
from jax import lax
from jax.experimental import pallas as pl
from jax.experimental.pallas import tpu as pltpu

D_MODEL = 2048
D_RET = D_MODEL // 2
D_SGU = D_MODEL - D_RET
RET_HEADS = 8
HEAD_DIM = D_RET // RET_HEADS
SGU_GROUPS = 8
GROUP_DIM = D_SGU // SGU_GROUPS
CHUNK = 128
D_FF = 5632
ROPE_BASE = 10000.0
EPS = 1e-6
D_PROJ = 4 * D_RET + 2 * D_SGU

F32 = jnp.float32
BF16 = jnp.bfloat16

VMEM_LIMIT_BYTES = 56 * 1024 * 1024

FFN_TM = 512
FFN_TF = 512
PROJ_TM = 512
PROJ_TN = 1536
MIX_TM = 512
OUT_TM = 512


def _rms_scale(v):
    return lax.rsqrt(jnp.mean(v * v, axis=-1, keepdims=True) + EPS)


def _ffn_kernel(x_ref, pre_g_ref, wg_ref, wu_ref, wd_ref, post_g_ref, o_ref, xn_ref, acc_ref):
    j = pl.program_id(1)

    @pl.when(j == 0)
    def _():
        x = x_ref[...]
        xn_ref[...] = ((x * _rms_scale(x)) * pre_g_ref[...]).astype(BF16)
        acc_ref[...] = jnp.zeros_like(acc_ref)

    xn = xn_ref[...]
    gate = jnp.dot(xn, wg_ref[...], preferred_element_type=F32)
    up = jnp.dot(xn, wu_ref[...], preferred_element_type=F32)
    h = (gate * jax.nn.sigmoid(gate) * up).astype(BF16)
    acc_ref[...] += jnp.dot(h, wd_ref[...], preferred_element_type=F32)

    @pl.when(j == pl.num_programs(1) - 1)
    def _():
        y = acc_ref[...]
        o_ref[...] = x_ref[...] + 0.5 * ((y * _rms_scale(y)) * post_g_ref[...])


def _ffn(x2d, pre_g, w_gate, w_up, w_down, post_g):
    t = x2d.shape[0]
    grid = (t // FFN_TM, D_FF // FFN_TF)
    return pl.pallas_call(
        _ffn_kernel,
        grid=grid,
        in_specs=[
            pl.BlockSpec((FFN_TM, D_MODEL), lambda i, j: (i, 0)),
            pl.BlockSpec((1, D_MODEL), lambda i, j: (0, 0)),
            pl.BlockSpec((D_MODEL, FFN_TF), lambda i, j: (0, j)),
            pl.BlockSpec((D_MODEL, FFN_TF), lambda i, j: (0, j)),
            pl.BlockSpec((FFN_TF, D_MODEL), lambda i, j: (j, 0)),
            pl.BlockSpec((1, D_MODEL), lambda i, j: (0, 0)),
        ],
        out_specs=pl.BlockSpec((FFN_TM, D_MODEL), lambda i, j: (i, 0)),
        out_shape=jax.ShapeDtypeStruct((t, D_MODEL), F32),
        scratch_shapes=[
            pltpu.VMEM((FFN_TM, D_MODEL), BF16),
            pltpu.VMEM((FFN_TM, D_MODEL), F32),
        ],
        compiler_params=pltpu.CompilerParams(
            dimension_semantics=("parallel", "arbitrary"),
            vmem_limit_bytes=VMEM_LIMIT_BYTES,
        ),
        name="ffn_half_step",
    )(x2d, pre_g, w_gate, w_up, w_down, post_g)


def _in_proj_kernel(x_ref, g_ref, w_ref, z_ref, xn_ref):
    @pl.when(pl.program_id(1) == 0)
    def _():
        x = x_ref[...]
        xn_ref[...] = ((x * _rms_scale(x)) * g_ref[...]).astype(BF16)

    z_ref[...] = jnp.dot(xn_ref[...], w_ref[...], preferred_element_type=F32).astype(BF16)


def _in_proj(x2d, g, w_in):
    t = x2d.shape[0]
    grid = (t // PROJ_TM, D_PROJ // PROJ_TN)
    return pl.pallas_call(
        _in_proj_kernel,
        grid=grid,
        in_specs=[
            pl.BlockSpec((PROJ_TM, D_MODEL), lambda i, j: (i, 0)),
            pl.BlockSpec((1, D_MODEL), lambda i, j: (0, 0)),
            pl.BlockSpec((D_MODEL, PROJ_TN), lambda i, j: (0, j)),
        ],
        out_specs=pl.BlockSpec((PROJ_TM, PROJ_TN), lambda i, j: (i, j)),
        out_shape=jax.ShapeDtypeStruct((t, D_PROJ), BF16),
        scratch_shapes=[pltpu.VMEM((PROJ_TM, D_MODEL), BF16)],
        compiler_params=pltpu.CompilerParams(
            dimension_semantics=("parallel", "arbitrary"),
            vmem_limit_bytes=VMEM_LIMIT_BYTES,
        ),
        name="mix_in_proj",
    )(x2d, g, w_in)


def _log_gamma(h):
    return math.log1p(-(2.0 ** (-5.0 - h)))


def _layernorm_lanes(v):
    mu = jnp.mean(v, axis=-1, keepdims=True)
    vc = v - mu
    return vc * lax.rsqrt(jnp.mean(vc * vc, axis=-1, keepdims=True) + EPS)


def _gelu_tanh(v):
    c = math.sqrt(2.0 / math.pi)
    return 0.5 * v * (1.0 + jnp.tanh(c * (v + 0.044715 * (v * v * v))))


def _mix_kernel(z_ref, pos_ref, invf_ref, rg_ref, sg_ref, ws_ref, bs_ref, o_ref, state_ref):
    @pl.when(pl.program_id(1) == 0)
    def _():
        state_ref[...] = jnp.zeros_like(state_ref)

    n_chunks = MIX_TM // CHUNK
    row = lax.broadcasted_iota(jnp.int32, (CHUNK, CHUNK), 0)
    col = lax.broadcasted_iota(jnp.int32, (CHUNK, CHUNK), 1)
    causal = row >= col
    rel = jnp.where(causal, row - col, 0).astype(F32)
    row_f = row.astype(F32)

    ang = pos_ref[...].astype(F32) * invf_ref[...]
    lane = lax.broadcasted_iota(jnp.int32, (1, HEAD_DIM), 1)
    cos_t = jnp.cos(ang)
    sin_t = jnp.where(lane < HEAD_DIM // 2, -1.0, 1.0) * jnp.sin(ang)
    q_scale = HEAD_DIM ** -0.5

    for h in range(RET_HEADS):
        lg = _log_gamma(h)
        decay_in = jnp.where(causal, jnp.exp(lg * rel), 0.0) * q_scale
        xi = jnp.exp(lg * (row_f + 1.0)) * q_scale
        zeta = jnp.exp(lg * (CHUNK - 1.0 - row_f))
        chunk_decay = math.exp(lg * CHUNK)
        gain = rg_ref[:, h * HEAD_DIM:(h + 1) * HEAD_DIM]

        def head_cols(base):
            return z_ref[:, base + h * HEAD_DIM: base + (h + 1) * HEAD_DIM]

        q = head_cols(0).astype(F32)
        k = head_cols(D_RET).astype(F32)
        v = head_cols(2 * D_RET)
        gate = head_cols(3 * D_RET).astype(F32)
        q = q * cos_t + pltpu.roll(q, HEAD_DIM // 2, 1) * sin_t
        k = k * cos_t + pltpu.roll(k, HEAD_DIM // 2, 1) * sin_t

        state = state_ref[h]
        for c in range(n_chunks):
            rows = slice(c * CHUNK, (c + 1) * CHUNK)
            qc, kc, vc = q[rows], k[rows], v[rows]
            scores = lax.dot_general(qc.astype(BF16), kc.astype(BF16), (((1,), (1,)), ((), ())),
                                     preferred_element_type=F32) * decay_in
            inner = jnp.dot(scores.astype(BF16), vc, preferred_element_type=F32)
            cross = jnp.dot((qc * xi).astype(BF16), state.astype(BF16), preferred_element_type=F32)
            kv = lax.dot_general((kc * zeta).astype(BF16), vc, (((0,), (0,)), ((), ())),
                                 preferred_element_type=F32)
            state = state * chunk_decay + kv
            ret = _layernorm_lanes(inner + cross) * gain
            gc = gate[rows]
            o_ref[rows, h * HEAD_DIM:(h + 1) * HEAD_DIM] = (ret * (gc * jax.nn.sigmoid(gc))).astype(BF16)
        state_ref[h] = state

    u_base = 4 * D_RET
    vs = _gelu_tanh(z_ref[:, u_base + D_SGU: u_base + 2 * D_SGU].astype(F32))
    vs = (_layernorm_lanes(vs) * sg_ref[...]).astype(BF16)
    for g in range(SGU_GROUPS):
        w = jnp.where(causal, ws_ref[g], 0.0).astype(BF16)
        bias = bs_ref[g]
        cols = slice(g * GROUP_DIM, (g + 1) * GROUP_DIM)
        for c in range(n_chunks):
            rows = slice(c * CHUNK, (c + 1) * CHUNK)
            mixed = jnp.dot(w, vs[rows, cols], preferred_element_type=F32) + bias
            u = _gelu_tanh(z_ref[rows, u_base + g * GROUP_DIM: u_base + (g + 1) * GROUP_DIM].astype(F32))
            o_ref[rows, D_RET + g * GROUP_DIM: D_RET + (g + 1) * GROUP_DIM] = (u * mixed).astype(BF16)


def _mix(z, positions, ret_norm_g, sgu_norm_g, w_s, b_s):
    b, s, _ = z.shape
    half = HEAD_DIM // 2
    inv_freq = ROPE_BASE ** (-jnp.arange(half, dtype=F32) / half)
    inv_freq = jnp.concatenate([inv_freq, inv_freq]).reshape(1, HEAD_DIM)
    grid = (b, s // MIX_TM)
    return pl.pallas_call(
        _mix_kernel,
        grid=grid,
        in_specs=[
            pl.BlockSpec((None, MIX_TM, D_PROJ), lambda i, j: (i, j, 0)),
            pl.BlockSpec((None, MIX_TM, 1), lambda i, j: (i, j, 0)),
            pl.BlockSpec((1, HEAD_DIM), lambda i, j: (0, 0)),
            pl.BlockSpec((1, D_RET), lambda i, j: (0, 0)),
            pl.BlockSpec((1, D_SGU), lambda i, j: (0, 0)),
            pl.BlockSpec((SGU_GROUPS, CHUNK, CHUNK), lambda i, j: (0, 0, 0)),
            pl.BlockSpec((SGU_GROUPS, CHUNK, 1), lambda i, j: (0, 0, 0)),
        ],
        out_specs=pl.BlockSpec((None, MIX_TM, D_MODEL), lambda i, j: (i, j, 0)),
        out_shape=jax.ShapeDtypeStruct((b, s, D_MODEL), BF16),
        scratch_shapes=[pltpu.VMEM((RET_HEADS, HEAD_DIM, HEAD_DIM), F32)],
        compiler_params=pltpu.CompilerParams(
            dimension_semantics=("parallel", "arbitrary"),
            vmem_limit_bytes=VMEM_LIMIT_BYTES,
        ),
        name="token_mixing",
    )(z, positions.reshape(b, s, 1), inv_freq, ret_norm_g, sgu_norm_g, w_s, b_s.reshape(SGU_GROUPS, CHUNK, 1))


def _out_proj_kernel(m_ref, w_ref, g_ref, x_ref, o_ref):
    y = jnp.dot(m_ref[...], w_ref[...], preferred_element_type=F32)
    o_ref[...] = x_ref[...] + (y * _rms_scale(y)) * g_ref[...]


def _out_proj(mix2d, w_out, g, x2d):
    t = x2d.shape[0]
    return pl.pallas_call(
        _out_proj_kernel,
        grid=(t // OUT_TM,),
        in_specs=[
            pl.BlockSpec((OUT_TM, D_MODEL), lambda i: (i, 0)),
            pl.BlockSpec((D_MODEL, D_MODEL), lambda i: (0, 0)),
            pl.BlockSpec((1, D_MODEL), lambda i: (0, 0)),
            pl.BlockSpec((OUT_TM, D_MODEL), lambda i: (i, 0)),
        ],
        out_specs=pl.BlockSpec((OUT_TM, D_MODEL), lambda i: (i, 0)),
        out_shape=jax.ShapeDtypeStruct((t, D_MODEL), F32),
        compiler_params=pltpu.CompilerParams(
            dimension_semantics=("parallel",),
            vmem_limit_bytes=VMEM_LIMIT_BYTES,
        ),
        name="mix_out_proj",
    )(mix2d, w_out, g, x2d)


def kernel(x, positions, ffn1_pre_g, ffn1_w_gate, ffn1_w_up, ffn1_w_down, ffn1_post_g, mix_pre_g, w_in, ret_norm_g, sgu_norm_g, sgu_w_s, sgu_b_s, w_out, mix_post_g, ffn2_pre_g, ffn2_w_gate, ffn2_w_up, ffn2_w_down, ffn2_post_g):
    b, s, d = x.shape
    depth = ffn1_pre_g.shape[0]
    h = x.reshape(b * s, d)
    for l in range(depth):
        h = _ffn(h, ffn1_pre_g[l][None], ffn1_w_gate[l].astype(BF16), ffn1_w_up[l].astype(BF16),
                 ffn1_w_down[l].astype(BF16), ffn1_post_g[l][None])
        z = _in_proj(h, mix_pre_g[l][None], w_in[l].astype(BF16))
        mixed = _mix(z.reshape(b, s, D_PROJ), positions, ret_norm_g[l][None], sgu_norm_g[l][None],
                     sgu_w_s[l], sgu_b_s[l])
        h = _out_proj(mixed.reshape(b * s, d), w_out[l].astype(BF16), mix_post_g[l][None], h)
        h = _ffn(h, ffn2_pre_g[l][None], ffn2_w_gate[l].astype(BF16), ffn2_w_up[l].astype(BF16),
                 ffn2_w_down[l].astype(BF16), ffn2_post_g[l][None])
    return h.reshape(b, s, d)
```

```python
import functools
import math

import jax
import jax.numpy as jnp
from jax import lax
from jax.experimental import pallas as pl
from jax.experimental.pallas import tpu as pltpu

D_MODEL = 2048
D_RET = D_MODEL // 2
D_SGU = D_MODEL - D_RET
RET_HEADS = 8
HEAD_DIM = D_RET // RET_HEADS
SGU_GROUPS = 8
GROUP_DIM = D_SGU // SGU_GROUPS
CHUNK = 128
D_FF = 5632
ROPE_BASE = 10000.0
EPS = 1e-6
D_PROJ = 4 * D_RET + 2 * D_SGU

F32 = jnp.float32
BF16 = jnp.bfloat16

VMEM_LIMIT_BYTES = 56 * 1024 * 1024

FFN_TM = 1024
FFN_TF = 512
FFN_OUT_ROWS = 256
PROJ_TM = 1024
PROJ_TN = 1536
MIX_TM = 512
OUT_TM = 512
LANES = 128
ROW_BLOCK = 128


def _rms_scale(v):
    return lax.rsqrt(jnp.mean(v * v, axis=-1, keepdims=True) + EPS)


def _store_row_scales(v_ref, scale_ref):
    scale_ref[...] = jnp.broadcast_to(_rms_scale(v_ref[...]), scale_ref.shape)


def _rmsnorm_rows_to_bf16(x_ref, g_ref, xn_ref, scale_ref):
    _store_row_scales(x_ref, scale_ref)
    for rb in range(x_ref.shape[0] // ROW_BLOCK):
        rows = slice(rb * ROW_BLOCK, (rb + 1) * ROW_BLOCK)
        scale = scale_ref[rows, :]
        for cb in range(x_ref.shape[1] // LANES):
            cols = slice(cb * LANES, (cb + 1) * LANES)
            xn_ref[rows, cols] = ((x_ref[rows, cols] * scale) * g_ref[:, cols]).astype(BF16)


def _ffn_kernel(x_ref, pre_g_ref, wg_ref, wu_ref, wd_ref, post_g_ref, o_hbm, xn_ref, acc_ref,
                scale_ref, stage_ref, out_sem):
    i = pl.program_id(0)
    j = pl.program_id(1)

    @pl.when(j == 0)
    def _():
        _rmsnorm_rows_to_bf16(x_ref, pre_g_ref, xn_ref, scale_ref)

    @pl.when((i == 0) & (j == 0))
    def _():
        acc_ref[...] = jnp.zeros_like(acc_ref)

    xn = xn_ref[...]
    gate = jnp.dot(xn, wg_ref[...], preferred_element_type=F32)
    up = jnp.dot(xn, wu_ref[...], preferred_element_type=F32)
    h = (gate * jax.nn.sigmoid(gate) * up).astype(BF16)
    carried = jnp.where(j == 0, 0.0, acc_ref[...])
    acc_ref[...] = carried + jnp.dot(h, wd_ref[...], preferred_element_type=F32)

    @pl.when(j == pl.num_programs(1) - 1)
    def _():
        _store_row_scales(acc_ref, scale_ref)
        half_g = 0.5 * post_g_ref[...]
        n_copies = FFN_TM // FFN_OUT_ROWS

        def out_copy(c):
            rows = pl.ds(i * FFN_TM + c * FFN_OUT_ROWS, FFN_OUT_ROWS)
            return pltpu.make_async_copy(stage_ref.at[c % 2], o_hbm.at[rows, :], out_sem.at[c % 2])

        for c in range(n_copies):
            if c >= 2:
                out_copy(c - 2).wait()
            for rb in range(FFN_OUT_ROWS // ROW_BLOCK):
                dst = slice(rb * ROW_BLOCK, (rb + 1) * ROW_BLOCK)
                rows = slice(c * FFN_OUT_ROWS + rb * ROW_BLOCK, c * FFN_OUT_ROWS + (rb + 1) * ROW_BLOCK)
                scale = scale_ref[rows, :]
                for cb in range(D_MODEL // LANES):
                    cols = slice(cb * LANES, (cb + 1) * LANES)
                    normed = (acc_ref[rows, cols] * scale) * half_g[:, cols]
                    stage_ref[c % 2, dst, cols] = x_ref[rows, cols] + normed
            out_copy(c).start()
        for c in range(n_copies - 2, n_copies):
            out_copy(c).wait()


def _ffn(x2d, pre_g, w_gate, w_up, w_down, post_g):
    t = x2d.shape[0]
    grid = (t // FFN_TM, D_FF // FFN_TF)
    return pl.pallas_call(
        _ffn_kernel,
        grid=grid,
        in_specs=[
            pl.BlockSpec((FFN_TM, D_MODEL), lambda i, j: (i, 0)),
            pl.BlockSpec((1, D_MODEL), lambda i, j: (0, 0)),
            pl.BlockSpec((D_MODEL, FFN_TF), lambda i, j: (0, j)),
            pl.BlockSpec((D_MODEL, FFN_TF), lambda i, j: (0, j)),
            pl.BlockSpec((FFN_TF, D_MODEL), lambda i, j: (j, 0)),
            pl.BlockSpec((1, D_MODEL), lambda i, j: (0, 0)),
        ],
        out_specs=pl.BlockSpec(memory_space=pl.ANY),
        out_shape=jax.ShapeDtypeStruct((t, D_MODEL), F32),
        scratch_shapes=[
            pltpu.VMEM((FFN_TM, D_MODEL), BF16),
            pltpu.VMEM((FFN_TM, D_MODEL), F32),
            pltpu.VMEM((FFN_TM, LANES), F32),
            pltpu.VMEM((2, FFN_OUT_ROWS, D_MODEL), F32),
            pltpu.SemaphoreType.DMA((2,)),
        ],
        compiler_params=pltpu.CompilerParams(
            dimension_semantics=("arbitrary", "arbitrary"),
            vmem_limit_bytes=VMEM_LIMIT_BYTES,
        ),
        name="ffn_half_step",
    )(x2d, pre_g, w_gate, w_up, w_down, post_g)


def _in_proj_kernel(x_ref, g_ref, w_ref, z_ref, xn_ref, scale_ref):
    @pl.when(pl.program_id(1) == 0)
    def _():
        _rmsnorm_rows_to_bf16(x_ref, g_ref, xn_ref, scale_ref)

    z_ref[...] = jnp.dot(xn_ref[...], w_ref[...], preferred_element_type=F32).astype(BF16)


def _in_proj(x2d, g, w_in):
    t = x2d.shape[0]
    grid = (t // PROJ_TM, D_PROJ // PROJ_TN)
    return pl.pallas_call(
        _in_proj_kernel,
        grid=grid,
        in_specs=[
            pl.BlockSpec((PROJ_TM, D_MODEL), lambda i, j: (i, 0)),
            pl.BlockSpec((1, D_MODEL), lambda i, j: (0, 0)),
            pl.BlockSpec((D_MODEL, PROJ_TN), lambda i, j: (0, j)),
        ],
        out_specs=pl.BlockSpec((PROJ_TM, PROJ_TN), lambda i, j: (i, j)),
        out_shape=jax.ShapeDtypeStruct((t, D_PROJ), BF16),
        scratch_shapes=[pltpu.VMEM((PROJ_TM, D_MODEL), BF16), pltpu.VMEM((PROJ_TM, LANES), F32)],
        compiler_params=pltpu.CompilerParams(
            dimension_semantics=("parallel", "arbitrary"),
            vmem_limit_bytes=VMEM_LIMIT_BYTES,
        ),
        name="mix_in_proj",
    )(x2d, g, w_in)


def _log_gamma(h):
    return math.log1p(-(2.0 ** (-5.0 - h)))


def _layernorm_lanes(v):
    mu = jnp.mean(v, axis=-1, keepdims=True)
    vc = v - mu
    return vc * lax.rsqrt(jnp.mean(vc * vc, axis=-1, keepdims=True) + EPS)


def _gelu_tanh(v):
    c = math.sqrt(2.0 / math.pi)
    return 0.5 * v * (1.0 + jnp.tanh(c * (v + 0.044715 * (v * v * v))))


def _mix_kernel(z_ref, pos_ref, invf_ref, rg_ref, sg_ref, ws_ref, bs_ref, o_ref, state_ref):
    @pl.when(pl.program_id(1) == 0)
    def _():
        state_ref[...] = jnp.zeros_like(state_ref)

    n_chunks = MIX_TM // CHUNK
    row = lax.broadcasted_iota(jnp.int32, (CHUNK, CHUNK), 0)
    col = lax.broadcasted_iota(jnp.int32, (CHUNK, CHUNK), 1)
    causal = row >= col
    rel = jnp.where(causal, row - col, 0).astype(F32)
    row_f = row.astype(F32)

    ang = pos_ref[...].astype(F32) * invf_ref[...]
    lane = lax.broadcasted_iota(jnp.int32, (1, HEAD_DIM), 1)
    cos_t = jnp.cos(ang)
    sin_t = jnp.where(lane < HEAD_DIM // 2, -1.0, 1.0) * jnp.sin(ang)
    q_scale = HEAD_DIM ** -0.5

    for h in range(RET_HEADS):
        lg = _log_gamma(h)
        decay_in = jnp.where(causal, jnp.exp(lg * rel), 0.0) * q_scale
        xi = jnp.exp(lg * (row_f + 1.0)) * q_scale
        zeta = jnp.exp(lg * (CHUNK - 1.0 - row_f))
        chunk_decay = math.exp(lg * CHUNK)
        gain = rg_ref[:, h * HEAD_DIM:(h + 1) * HEAD_DIM]

        def head_cols(base):
            return z_ref[:, base + h * HEAD_DIM: base + (h + 1) * HEAD_DIM]

        q = head_cols(0).astype(F32)
        k = head_cols(D_RET).astype(F32)
        v = head_cols(2 * D_RET)
        gate = head_cols(3 * D_RET).astype(F32)
        q = q * cos_t + pltpu.roll(q, HEAD_DIM // 2, 1) * sin_t
        k = k * cos_t + pltpu.roll(k, HEAD_DIM // 2, 1) * sin_t

        state = state_ref[h]
        for c in range(n_chunks):
            rows = slice(c * CHUNK, (c + 1) * CHUNK)
            qc, kc, vc = q[rows], k[rows], v[rows]
            scores = lax.dot_general(qc.astype(BF16), kc.astype(BF16), (((1,), (1,)), ((), ())),
                                     preferred_element_type=F32) * decay_in
            inner = jnp.dot(scores.astype(BF16), vc, preferred_element_type=F32)
            cross = jnp.dot((qc * xi).astype(BF16), state.astype(BF16), preferred_element_type=F32)
            kv = lax.dot_general((kc * zeta).astype(BF16), vc, (((0,), (0,)), ((), ())),
                                 preferred_element_type=F32)
            state = state * chunk_decay + kv
            ret = _layernorm_lanes(inner + cross) * gain
            gc = gate[rows]
            o_ref[rows, h * HEAD_DIM:(h + 1) * HEAD_DIM] = (ret * (gc * jax.nn.sigmoid(gc))).astype(BF16)
        state_ref[h] = state

    u_base = 4 * D_RET
    vs = _gelu_tanh(z_ref[:, u_base + D_SGU: u_base + 2 * D_SGU].astype(F32))
    vs = (_layernorm_lanes(vs) * sg_ref[...]).astype(BF16)
    for g in range(SGU_GROUPS):
        w = jnp.where(causal, ws_ref[g], 0.0).astype(BF16)
        bias = bs_ref[g]
        cols = slice(g * GROUP_DIM, (g + 1) * GROUP_DIM)
        for c in range(n_chunks):
            rows = slice(c * CHUNK, (c + 1) * CHUNK)
            mixed = jnp.dot(w, vs[rows, cols], preferred_element_type=F32) + bias
            u = _gelu_tanh(z_ref[rows, u_base + g * GROUP_DIM: u_base + (g + 1) * GROUP_DIM].astype(F32))
            o_ref[rows, D_RET + g * GROUP_DIM: D_RET + (g + 1) * GROUP_DIM] = (u * mixed).astype(BF16)


def _mix(z, positions, ret_norm_g, sgu_norm_g, w_s, b_s):
    b, s, _ = z.shape
    half = HEAD_DIM // 2
    inv_freq = ROPE_BASE ** (-jnp.arange(half, dtype=F32) / half)
    inv_freq = jnp.concatenate([inv_freq, inv_freq]).reshape(1, HEAD_DIM)
    grid = (b, s // MIX_TM)
    return pl.pallas_call(
        _mix_kernel,
        grid=grid,
        in_specs=[
            pl.BlockSpec((None, MIX_TM, D_PROJ), lambda i, j: (i, j, 0)),
            pl.BlockSpec((None, MIX_TM, 1), lambda i, j: (i, j, 0)),
            pl.BlockSpec((1, HEAD_DIM), lambda i, j: (0, 0)),
            pl.BlockSpec((1, D_RET), lambda i, j: (0, 0)),
            pl.BlockSpec((1, D_SGU), lambda i, j: (0, 0)),
            pl.BlockSpec((SGU_GROUPS, CHUNK, CHUNK), lambda i, j: (0, 0, 0)),
            pl.BlockSpec((SGU_GROUPS, CHUNK, 1), lambda i, j: (0, 0, 0)),
        ],
        out_specs=pl.BlockSpec((None, MIX_TM, D_MODEL), lambda i, j: (i, j, 0)),
        out_shape=jax.ShapeDtypeStruct((b, s, D_MODEL), BF16),
        scratch_shapes=[pltpu.VMEM((RET_HEADS, HEAD_DIM, HEAD_DIM), F32)],
        compiler_params=pltpu.CompilerParams(
            dimension_semantics=("parallel", "arbitrary"),
            vmem_limit_bytes=VMEM_LIMIT_BYTES,
        ),
        name="token_mixing",
    )(z, positions.reshape(b, s, 1), inv_freq, ret_norm_g, sgu_norm_g, w_s, b_s.reshape(SGU_GROUPS, CHUNK, 1))


def _out_proj_kernel(m_ref, w_ref, g_ref, x_ref, o_ref, y_ref, scale_ref):
    y_ref[...] = jnp.dot(m_ref[...], w_ref[...], preferred_element_type=F32)
    _store_row_scales(y_ref, scale_ref)
    for rb in range(OUT_TM // ROW_BLOCK):
        rows = slice(rb * ROW_BLOCK, (rb + 1) * ROW_BLOCK)
        scale = scale_ref[rows, :]
        for cb in range(D_MODEL // LANES):
            cols = slice(cb * LANES, (cb + 1) * LANES)
            o_ref[rows, cols] = x_ref[rows, cols] + (y_ref[rows, cols] * scale) * g_ref[:, cols]


def _out_proj(mix2d, w_out, g, x2d):
    t = x2d.shape[0]
    return pl.pallas_call(
        _out_proj_kernel,
        grid=(t // OUT_TM,),
        in_specs=[
            pl.BlockSpec((OUT_TM, D_MODEL), lambda i: (i, 0)),
            pl.BlockSpec((D_MODEL, D_MODEL), lambda i: (0, 0)),
            pl.BlockSpec((1, D_MODEL), lambda i: (0, 0)),
            pl.BlockSpec((OUT_TM, D_MODEL), lambda i: (i, 0)),
        ],
        out_specs=pl.BlockSpec((OUT_TM, D_MODEL), lambda i: (i, 0)),
        out_shape=jax.ShapeDtypeStruct((t, D_MODEL), F32),
        scratch_shapes=[pltpu.VMEM((OUT_TM, D_MODEL), F32), pltpu.VMEM((OUT_TM, LANES), F32)],
        compiler_params=pltpu.CompilerParams(
            dimension_semantics=("parallel",),
            vmem_limit_bytes=VMEM_LIMIT_BYTES,
        ),
        name="mix_out_proj",
    )(mix2d, w_out, g, x2d)


def kernel(x, positions, ffn1_pre_g, ffn1_w_gate, ffn1_w_up, ffn1_w_down, ffn1_post_g, mix_pre_g, w_in, ret_norm_g, sgu_norm_g, sgu_w_s, sgu_b_s, w_out, mix_post_g, ffn2_pre_g, ffn2_w_gate, ffn2_w_up, ffn2_w_down, ffn2_post_g):
    b, s, d = x.shape
    depth = ffn1_pre_g.shape[0]
    h = x.reshape(b * s, d)
    for l in range(depth):
        h = _ffn(h, ffn1_pre_g[l][None], ffn1_w_gate[l].astype(BF16), ffn1_w_up[l].astype(BF16),
                 ffn1_w_down[l].astype(BF16), ffn1_post_g[l][None])
        z = _in_proj(h, mix_pre_g[l][None], w_in[l].astype(BF16))
        mixed = _mix(z.reshape(b, s, D_PROJ), positions, ret_norm_g[l][None], sgu_norm_g[l][None],
                     sgu_w_s[l], sgu_b_s[l])
        h = _out_proj(mixed.reshape(b * s, d), w_out[l].astype(BF16), mix_post_g[l][None], h)
        h = _ffn(h, ffn2_pre_g[l][None], ffn2_w_gate[l].astype(BF16), ffn2_w_up[l].astype(BF16),
                 ffn2_w_down[l].astype(BF16), ffn2_post_g[l][None])
    return h.reshape(b, s, d)
```

```python
import functools
import math

import jax
import jax.numpy as jnp
from jax import lax
from jax.experimental import pallas as pl
from jax.experimental.pallas import tpu as pltpu

D_MODEL = 2048
D_RET = D_MODEL // 2
D_SGU = D_MODEL - D_RET
RET_HEADS = 8
HEAD_DIM = D_RET // RET_HEADS
SGU_GROUPS = 8
GROUP_DIM = D_SGU // SGU_GROUPS
CHUNK = 128
D_FF = 5632
ROPE_BASE = 10000.0
EPS = 1e-6
D_PROJ = 4 * D_RET + 2 * D_SGU

F32 = jnp.float32
BF16 = jnp.bfloat16

VMEM_LIMIT_BYTES = 56 * 1024 * 1024

FFN_TM = 1024
FFN_TF = 512
FFN_OUT_ROWS = 256
MIX_TM = 512
MIX_SLAB = 256
OUT_TM = 512
LANES = 128
ROW_BLOCK = 128


def _rms_scale(v):
    return lax.rsqrt(jnp.mean(v * v, axis=-1, keepdims=True) + EPS)


def _store_row_scales(v_ref, scale_ref):
    scale_ref[...] = jnp.broadcast_to(_rms_scale(v_ref[...]), scale_ref.shape)


def _rmsnorm_rows_to_bf16(x_ref, g_ref, xn_ref, scale_ref):
    _store_row_scales(x_ref, scale_ref)
    for rb in range(x_ref.shape[0] // ROW_BLOCK):
        rows = slice(rb * ROW_BLOCK, (rb + 1) * ROW_BLOCK)
        scale = scale_ref[rows, :]
        for cb in range(x_ref.shape[1] // LANES):
            cols = slice(cb * LANES, (cb + 1) * LANES)
            xn_ref[rows, cols] = ((x_ref[rows, cols] * scale) * g_ref[:, cols]).astype(BF16)


def _ffn_kernel(x_ref, pre_g_ref, wg_ref, wu_ref, wd_ref, post_g_ref, o_hbm, xn_ref, acc_ref,
                scale_ref, stage_ref, out_sem):
    i = pl.program_id(0)
    j = pl.program_id(1)

    @pl.when(j == 0)
    def _():
        _rmsnorm_rows_to_bf16(x_ref, pre_g_ref, xn_ref, scale_ref)

    @pl.when((i == 0) & (j == 0))
    def _():
        acc_ref[...] = jnp.zeros_like(acc_ref)

    xn = xn_ref[...]
    gate = jnp.dot(xn, wg_ref[...], preferred_element_type=F32)
    up = jnp.dot(xn, wu_ref[...], preferred_element_type=F32)
    h = (gate * jax.nn.sigmoid(gate) * up).astype(BF16)
    carried = jnp.where(j == 0, 0.0, acc_ref[...])
    acc_ref[...] = carried + jnp.dot(h, wd_ref[...], preferred_element_type=F32)

    @pl.when(j == pl.num_programs(1) - 1)
    def _():
        n_copies = FFN_TM // FFN_OUT_ROWS

        def out_copy(tile, c):
            rows = pl.ds(tile * FFN_TM + c * FFN_OUT_ROWS, FFN_OUT_ROWS)
            return pltpu.make_async_copy(stage_ref.at[c], o_hbm.at[rows, :], out_sem.at[c])

        @pl.when(i > 0)
        def _():
            for c in range(n_copies):
                out_copy(i - 1, c).wait()

        _store_row_scales(acc_ref, scale_ref)
        half_g = 0.5 * post_g_ref[...]
        for c in range(n_copies):
            for rb in range(FFN_OUT_ROWS // ROW_BLOCK):
                dst = slice(rb * ROW_BLOCK, (rb + 1) * ROW_BLOCK)
                rows = slice(c * FFN_OUT_ROWS + rb * ROW_BLOCK, c * FFN_OUT_ROWS + (rb + 1) * ROW_BLOCK)
                scale = scale_ref[rows, :]
                for cb in range(D_MODEL // LANES):
                    cols = slice(cb * LANES, (cb + 1) * LANES)
                    normed = (acc_ref[rows, cols] * scale) * half_g[:, cols]
                    stage_ref[c, dst, cols] = x_ref[rows, cols] + normed
            out_copy(i, c).start()

        @pl.when(i == pl.num_programs(0) - 1)
        def _():
            for c in range(n_copies):
                out_copy(i, c).wait()


def _ffn(x2d, pre_g, w_gate, w_up, w_down, post_g):
    t = x2d.shape[0]
    grid = (t // FFN_TM, D_FF // FFN_TF)
    return pl.pallas_call(
        _ffn_kernel,
        grid=grid,
        in_specs=[
            pl.BlockSpec((FFN_TM, D_MODEL), lambda i, j: (i, 0)),
            pl.BlockSpec((1, D_MODEL), lambda i, j: (0, 0)),
            pl.BlockSpec((D_MODEL, FFN_TF), lambda i, j: (0, j)),
            pl.BlockSpec((D_MODEL, FFN_TF), lambda i, j: (0, j)),
            pl.BlockSpec((FFN_TF, D_MODEL), lambda i, j: (j, 0)),
            pl.BlockSpec((1, D_MODEL), lambda i, j: (0, 0)),
        ],
        out_specs=pl.BlockSpec(memory_space=pl.ANY),
        out_shape=jax.ShapeDtypeStruct((t, D_MODEL), F32),
        scratch_shapes=[
            pltpu.VMEM((FFN_TM, D_MODEL), BF16),
            pltpu.VMEM((FFN_TM, D_MODEL), F32),
            pltpu.VMEM((FFN_TM, LANES), F32),
            pltpu.VMEM((FFN_TM // FFN_OUT_ROWS, FFN_OUT_ROWS, D_MODEL), F32),
            pltpu.SemaphoreType.DMA((FFN_TM // FFN_OUT_ROWS,)),
        ],
        compiler_params=pltpu.CompilerParams(
            dimension_semantics=("arbitrary", "arbitrary"),
            vmem_limit_bytes=VMEM_LIMIT_BYTES,
        ),
        name="ffn_half_step",
    )(x2d, pre_g, w_gate, w_up, w_down, post_g)


def _log_gamma(h):
    return math.log1p(-(2.0 ** (-5.0 - h)))


def _layernorm_lanes(v):
    mu = jnp.mean(v, axis=-1, keepdims=True)
    vc = v - mu
    return vc * lax.rsqrt(jnp.mean(vc * vc, axis=-1, keepdims=True) + EPS)


def _gelu_tanh(v):
    c = math.sqrt(2.0 / math.pi)
    return 0.5 * v * (1.0 + jnp.tanh(c * (v + 0.044715 * (v * v * v))))


def _mix_kernel(tiles_per_row, x_ref, pos_ref, pre_g_ref, w_ref, invf_ref, rg_ref, sg_ref, ws_ref,
                bs_ref, o_ref, xn_ref, xn_next_ref, scale_ref, state_ref, vs_ref, vsn_ref):
    s = pl.program_id(0)

    @pl.when(s == 0)
    def _():
        _rmsnorm_rows_to_bf16(x_ref, pre_g_ref, xn_ref, scale_ref)

    @pl.when((s > 0) & ((s - 1) % tiles_per_row == 0))
    def _():
        state_ref[...] = jnp.zeros_like(state_ref)

    @pl.when(s > 0)
    def _():
        xn = xn_ref[...]

        def project(first_col):
            return jnp.dot(xn, w_ref[:, first_col:first_col + MIX_SLAB], preferred_element_type=F32)

        u_base = 4 * D_RET
        for p in range(D_SGU // MIX_SLAB):
            vs_ref[:, p * MIX_SLAB:(p + 1) * MIX_SLAB] = _gelu_tanh(project(u_base + D_SGU + p * MIX_SLAB))

        _rmsnorm_rows_to_bf16(x_ref, pre_g_ref, xn_next_ref, scale_ref)
        vsn_ref[...] = (_layernorm_lanes(vs_ref[...]) * sg_ref[...]).astype(BF16)

        n_chunks = MIX_TM // CHUNK
        row = lax.broadcasted_iota(jnp.int32, (CHUNK, CHUNK), 0)
        col = lax.broadcasted_iota(jnp.int32, (CHUNK, CHUNK), 1)
        causal = row >= col
        rel = jnp.where(causal, row - col, 0).astype(F32)
        row_f = row.astype(F32)

        ang = pos_ref[...].astype(F32) * invf_ref[...]
        lane = lax.broadcasted_iota(jnp.int32, (1, HEAD_DIM), 1)
        cos_t = jnp.cos(ang)
        sin_t = jnp.where(lane < HEAD_DIM // 2, -1.0, 1.0) * jnp.sin(ang)
        q_scale = HEAD_DIM ** -0.5
        heads_per_slab = MIX_SLAB // HEAD_DIM

        for p in range(RET_HEADS // heads_per_slab):
            q2 = project(p * MIX_SLAB)
            k2 = project(D_RET + p * MIX_SLAB)
            v2 = project(2 * D_RET + p * MIX_SLAB).astype(BF16)
            g2 = project(3 * D_RET + p * MIX_SLAB)
            for hh in range(heads_per_slab):
                h = p * heads_per_slab + hh
                lanes = slice(hh * HEAD_DIM, (hh + 1) * HEAD_DIM)
                lg = _log_gamma(h)
                decay_in = jnp.where(causal, jnp.exp(lg * rel), 0.0) * q_scale
                xi = jnp.exp(lg * (row_f + 1.0)) * q_scale
                zeta = jnp.exp(lg * (CHUNK - 1.0 - row_f))
                chunk_decay = math.exp(lg * CHUNK)
                gain = rg_ref[:, h * HEAD_DIM:(h + 1) * HEAD_DIM]

                q = q2[:, lanes]
                k = k2[:, lanes]
                v = v2[:, lanes]
                gate = g2[:, lanes]
                q = q * cos_t + pltpu.roll(q, HEAD_DIM // 2, 1) * sin_t
                k = k * cos_t + pltpu.roll(k, HEAD_DIM // 2, 1) * sin_t

                state = state_ref[h]
                for c in range(n_chunks):
                    rows = slice(c * CHUNK, (c + 1) * CHUNK)
                    qc, kc, vc = q[rows], k[rows], v[rows]
                    scores = lax.dot_general(qc.astype(BF16), kc.astype(BF16), (((1,), (1,)), ((), ())),
                                             preferred_element_type=F32) * decay_in
                    inner = jnp.dot(scores.astype(BF16), vc, preferred_element_type=F32)
                    cross = jnp.dot((qc * xi).astype(BF16), state.astype(BF16), preferred_element_type=F32)
                    kv = lax.dot_general((kc * zeta).astype(BF16), vc, (((0,), (0,)), ((), ())),
                                         preferred_element_type=F32)
                    state = state * chunk_decay + kv
                    ret = _layernorm_lanes(inner + cross) * gain
                    gc = gate[rows]
                    o_ref[rows, h * HEAD_DIM:(h + 1) * HEAD_DIM] = (ret * (gc * jax.nn.sigmoid(gc))).astype(BF16)
                state_ref[h] = state

        groups_per_slab = MIX_SLAB // GROUP_DIM
        for p in range(SGU_GROUPS // groups_per_slab):
            u2 = _gelu_tanh(project(u_base + p * MIX_SLAB))
            for gg in range(groups_per_slab):
                g = p * groups_per_slab + gg
                w = jnp.where(causal, ws_ref[g], 0.0).astype(BF16)
                bias = bs_ref[g]
                cols = slice(g * GROUP_DIM, (g + 1) * GROUP_DIM)
                for c in range(n_chunks):
                    rows = slice(c * CHUNK, (c + 1) * CHUNK)
                    mixed = jnp.dot(w, vsn_ref[rows, cols], preferred_element_type=F32) + bias
                    u = u2[rows, gg * GROUP_DIM:(gg + 1) * GROUP_DIM]
                    o_ref[rows, D_RET + g * GROUP_DIM: D_RET + (g + 1) * GROUP_DIM] = (u * mixed).astype(BF16)

        xn_ref[...] = xn_next_ref[...]


def _mix(x2d, positions, pre_g, w_in, ret_norm_g, sgu_norm_g, w_s, b_s):
    t = x2d.shape[0]
    seq = positions.shape[1]
    n_tiles = t // MIX_TM
    half = HEAD_DIM // 2
    inv_freq = ROPE_BASE ** (-jnp.arange(half, dtype=F32) / half)
    inv_freq = jnp.concatenate([inv_freq, inv_freq]).reshape(1, HEAD_DIM)

    def const(shape):
        return pl.BlockSpec(shape, lambda s: (0,) * len(shape))

    return pl.pallas_call(
        functools.partial(_mix_kernel, seq // MIX_TM),
        grid=(n_tiles + 1,),
        in_specs=[
            pl.BlockSpec((MIX_TM, D_MODEL), lambda s: (jnp.minimum(s, n_tiles - 1), 0)),
            pl.BlockSpec((MIX_TM, 1), lambda s: (jnp.maximum(s - 1, 0), 0)),
            const((1, D_MODEL)),
            pl.BlockSpec((D_MODEL, D_PROJ), lambda s: (0, 0), pipeline_mode=pl.Buffered(1)),
            const((1, HEAD_DIM)),
            const((1, D_RET)),
            const((1, D_SGU)),
            const((SGU_GROUPS, CHUNK, CHUNK)),
            const((SGU_GROUPS, CHUNK, 1)),
        ],
        out_specs=pl.BlockSpec((MIX_TM, D_MODEL), lambda s: (jnp.maximum(s - 1, 0), 0)),
        out_shape=jax.ShapeDtypeStruct((t, D_MODEL), BF16),
        scratch_shapes=[
            pltpu.VMEM((MIX_TM, D_MODEL), BF16),
            pltpu.VMEM((MIX_TM, D_MODEL), BF16),
            pltpu.VMEM((MIX_TM, LANES), F32),
            pltpu.VMEM((RET_HEADS, HEAD_DIM, HEAD_DIM), F32),
            pltpu.VMEM((MIX_TM, D_SGU), F32),
            pltpu.VMEM((MIX_TM, D_SGU), BF16),
        ],
        compiler_params=pltpu.CompilerParams(
            dimension_semantics=("arbitrary",),
            vmem_limit_bytes=VMEM_LIMIT_BYTES,
        ),
        name="token_mixing",
    )(x2d, positions.reshape(t, 1), pre_g, w_in, inv_freq, ret_norm_g, sgu_norm_g, w_s,
      b_s.reshape(SGU_GROUPS, CHUNK, 1))


def _out_proj_kernel(m_ref, w_ref, g_ref, x_ref, o_ref, y_ref, scale_ref):
    y_ref[...] = jnp.dot(m_ref[...], w_ref[...], preferred_element_type=F32)
    _store_row_scales(y_ref, scale_ref)
    for rb in range(OUT_TM // ROW_BLOCK):
        rows = slice(rb * ROW_BLOCK, (rb + 1) * ROW_BLOCK)
        scale = scale_ref[rows, :]
        for cb in range(D_MODEL // LANES):
            cols = slice(cb * LANES, (cb + 1) * LANES)
            o_ref[rows, cols] = x_ref[rows, cols] + (y_ref[rows, cols] * scale) * g_ref[:, cols]


def _out_proj(mix2d, w_out, g, x2d):
    t = x2d.shape[0]
    return pl.pallas_call(
        _out_proj_kernel,
        grid=(t // OUT_TM,),
        in_specs=[
            pl.BlockSpec((OUT_TM, D_MODEL), lambda i: (i, 0)),
            pl.BlockSpec((D_MODEL, D_MODEL), lambda i: (0, 0)),
            pl.BlockSpec((1, D_MODEL), lambda i: (0, 0)),
            pl.BlockSpec((OUT_TM, D_MODEL), lambda i: (i, 0)),
        ],
        out_specs=pl.BlockSpec((OUT_TM, D_MODEL), lambda i: (i, 0)),
        out_shape=jax.ShapeDtypeStruct((t, D_MODEL), F32),
        scratch_shapes=[pltpu.VMEM((OUT_TM, D_MODEL), F32), pltpu.VMEM((OUT_TM, LANES), F32)],
        compiler_params=pltpu.CompilerParams(
            dimension_semantics=("parallel",),
            vmem_limit_bytes=VMEM_LIMIT_BYTES,
        ),
        name="mix_out_proj",
    )(mix2d, w_out, g, x2d)


def kernel(x, positions, ffn1_pre_g, ffn1_w_gate, ffn1_w_up, ffn1_w_down, ffn1_post_g, mix_pre_g, w_in, ret_norm_g, sgu_norm_g, sgu_w_s, sgu_b_s, w_out, mix_post_g, ffn2_pre_g, ffn2_w_gate, ffn2_w_up, ffn2_w_down, ffn2_post_g):
    b, s, d = x.shape
    depth = ffn1_pre_g.shape[0]
    h = x.reshape(b * s, d)
    for l in range(depth):
        h = _ffn(h, ffn1_pre_g[l][None], ffn1_w_gate[l].astype(BF16), ffn1_w_up[l].astype(BF16),
                 ffn1_w_down[l].astype(BF16), ffn1_post_g[l][None])
        mixed = _mix(h, positions, mix_pre_g[l][None], w_in[l].astype(BF16), ret_norm_g[l][None],
                     sgu_norm_g[l][None], sgu_w_s[l], sgu_b_s[l])
        h = _out_proj(mixed, w_out[l].astype(BF16), mix_post_g[l][None], h)
        h = _ffn(h, ffn2_pre_g[l][None], ffn2_w_gate[l].astype(BF16), ffn2_w_up[l].astype(BF16),
                 ffn2_w_down[l].astype(BF16), ffn2_post_g[l][None])
    return h.reshape(b, s, d)
```

```python
import functools
import math

import jax
import jax.numpy as jnp
from jax import lax
from jax.experimental import pallas as pl
from jax.experimental.pallas import tpu as pltpu

D_MODEL = 2048
D_RET = D_MODEL // 2
D_SGU = D_MODEL - D_RET
RET_HEADS = 8
HEAD_DIM = D_RET // RET_HEADS
SGU_GROUPS = 8
GROUP_DIM = D_SGU // SGU_GROUPS
CHUNK = 128
D_FF = 5632
ROPE_BASE = 10000.0
EPS = 1e-6
D_PROJ = 4 * D_RET + 2 * D_SGU

F32 = jnp.float32
BF16 = jnp.bfloat16

VMEM_LIMIT_BYTES = 56 * 1024 * 1024

FFN_TM = 1024
FFN_TF = 512
FFN_OUT_ROWS = 256
MIX_TM = 512
MIX_SLAB = 512
OUT_TM = 512
LANES = 128
ROW_BLOCK = 128


def _rms_scale(v):
    return lax.rsqrt(jnp.mean(v * v, axis=-1, keepdims=True) + EPS)


def _store_row_scales(v_ref, scale_ref):
    scale_ref[...] = jnp.broadcast_to(_rms_scale(v_ref[...]), scale_ref.shape)


def _rmsnorm_rows_to_bf16(x_ref, g_ref, xn_ref, scale_ref):
    _store_row_scales(x_ref, scale_ref)
    for rb in range(x_ref.shape[0] // ROW_BLOCK):
        rows = slice(rb * ROW_BLOCK, (rb + 1) * ROW_BLOCK)
        scale = scale_ref[rows, :]
        for cb in range(x_ref.shape[1] // LANES):
            cols = slice(cb * LANES, (cb + 1) * LANES)
            xn_ref[rows, cols] = ((x_ref[rows, cols] * scale) * g_ref[:, cols]).astype(BF16)


def _ffn_kernel(x_ref, pre_g_ref, wg_ref, wu_ref, wd_ref, post_g_ref, o_hbm, xn_ref, acc_ref,
                scale_ref, stage_ref, out_sem):
    i = pl.program_id(0)
    j = pl.program_id(1)

    @pl.when(j == 0)
    def _():
        _rmsnorm_rows_to_bf16(x_ref, pre_g_ref, xn_ref, scale_ref)

    @pl.when((i == 0) & (j == 0))
    def _():
        acc_ref[...] = jnp.zeros_like(acc_ref)

    xn = xn_ref[...]
    gate = jnp.dot(xn, wg_ref[...], preferred_element_type=F32)
    up = jnp.dot(xn, wu_ref[...], preferred_element_type=F32)
    half_gate = 0.5 * gate
    gu = half_gate * up
    h = (gu + gu * jnp.tanh(half_gate)).astype(BF16)
    carried = jnp.where(j == 0, 0.0, acc_ref[...])
    acc_ref[...] = carried + jnp.dot(h, wd_ref[...], preferred_element_type=F32)

    @pl.when(j == pl.num_programs(1) - 1)
    def _():
        n_copies = FFN_TM // FFN_OUT_ROWS

        def out_copy(tile, c):
            rows = pl.ds(tile * FFN_TM + c * FFN_OUT_ROWS, FFN_OUT_ROWS)
            return pltpu.make_async_copy(stage_ref.at[c], o_hbm.at[rows, :], out_sem.at[c])

        @pl.when(i > 0)
        def _():
            for c in range(n_copies):
                out_copy(i - 1, c).wait()

        _store_row_scales(acc_ref, scale_ref)
        half_g = 0.5 * post_g_ref[...]
        for c in range(n_copies):
            for rb in range(FFN_OUT_ROWS // ROW_BLOCK):
                dst = slice(rb * ROW_BLOCK, (rb + 1) * ROW_BLOCK)
                rows = slice(c * FFN_OUT_ROWS + rb * ROW_BLOCK, c * FFN_OUT_ROWS + (rb + 1) * ROW_BLOCK)
                scale = scale_ref[rows, :]
                for cb in range(D_MODEL // LANES):
                    cols = slice(cb * LANES, (cb + 1) * LANES)
                    normed = (acc_ref[rows, cols] * scale) * half_g[:, cols]
                    stage_ref[c, dst, cols] = x_ref[rows, cols] + normed
            out_copy(i, c).start()

        @pl.when(i == pl.num_programs(0) - 1)
        def _():
            for c in range(n_copies):
                out_copy(i, c).wait()


def _ffn(x2d, pre_g, w_gate, w_up, w_down, post_g):
    t = x2d.shape[0]
    grid = (t // FFN_TM, D_FF // FFN_TF)
    return pl.pallas_call(
        _ffn_kernel,
        grid=grid,
        in_specs=[
            pl.BlockSpec((FFN_TM, D_MODEL), lambda i, j: (i, 0)),
            pl.BlockSpec((1, D_MODEL), lambda i, j: (0, 0)),
            pl.BlockSpec((D_MODEL, FFN_TF), lambda i, j: (0, j)),
            pl.BlockSpec((D_MODEL, FFN_TF), lambda i, j: (0, j)),
            pl.BlockSpec((FFN_TF, D_MODEL), lambda i, j: (j, 0)),
            pl.BlockSpec((1, D_MODEL), lambda i, j: (0, 0)),
        ],
        out_specs=pl.BlockSpec(memory_space=pl.ANY),
        out_shape=jax.ShapeDtypeStruct((t, D_MODEL), F32),
        scratch_shapes=[
            pltpu.VMEM((FFN_TM, D_MODEL), BF16),
            pltpu.VMEM((FFN_TM, D_MODEL), F32),
            pltpu.VMEM((FFN_TM, LANES), F32),
            pltpu.VMEM((FFN_TM // FFN_OUT_ROWS, FFN_OUT_ROWS, D_MODEL), F32),
            pltpu.SemaphoreType.DMA((FFN_TM // FFN_OUT_ROWS,)),
        ],
        compiler_params=pltpu.CompilerParams(
            dimension_semantics=("arbitrary", "arbitrary"),
            vmem_limit_bytes=VMEM_LIMIT_BYTES,
        ),
        name="ffn_half_step",
    )(x2d, pre_g, w_gate, w_up, w_down, post_g)


def _log_gamma(h):
    return math.log1p(-(2.0 ** (-5.0 - h)))


def _layernorm_lanes(v):
    mu = jnp.mean(v, axis=-1, keepdims=True)
    vc = v - mu
    return vc * lax.rsqrt(jnp.mean(vc * vc, axis=-1, keepdims=True) + EPS)


def _gelu_tanh(v):
    c = math.sqrt(2.0 / math.pi)
    neg2y = v * ((v * v) * (-2.0 * c * 0.044715) + (-2.0 * c))
    return v / (1.0 + jnp.exp(neg2y))


def _mix_kernel(tiles_per_row, x_ref, pos_ref, pre_g_ref, w_ref, invf_ref, rg_ref, sg_ref, ws_ref,
                bs_ref, o_ref, xn_ref, xn_next_ref, scale_ref, state_ref, vs_ref, vsn_ref):
    s = pl.program_id(0)

    @pl.when(s == 0)
    def _():
        _rmsnorm_rows_to_bf16(x_ref, pre_g_ref, xn_ref, scale_ref)

    @pl.when((s > 0) & ((s - 1) % tiles_per_row == 0))
    def _():
        state_ref[...] = jnp.zeros_like(state_ref)

    @pl.when(s > 0)
    def _():
        xn = xn_ref[...]

        def project(first_col):
            return jnp.dot(xn, w_ref[first_col // MIX_SLAB], preferred_element_type=F32)

        u_base = 4 * D_RET
        for p in range(D_SGU // MIX_SLAB):
            vs_ref[:, p * MIX_SLAB:(p + 1) * MIX_SLAB] = _gelu_tanh(project(u_base + D_SGU + p * MIX_SLAB))

        _rmsnorm_rows_to_bf16(x_ref, pre_g_ref, xn_next_ref, scale_ref)
        vsn_ref[...] = (_layernorm_lanes(vs_ref[...]) * sg_ref[...]).astype(BF16)

        lane = lax.broadcasted_iota(jnp.int32, (1, HEAD_DIM), 1)
        low = lane < HEAD_DIM // 2
        pos = pos_ref[...].astype(F32)
        ang = jnp.where(low, pos[:MIX_TM // 2], pos[MIX_TM // 2:]) * invf_ref[...]
        cos_p, sin_p = jnp.cos(ang), jnp.sin(ang)
        cos_r, sin_r = pltpu.roll(cos_p, HEAD_DIM // 2, 1), pltpu.roll(sin_p, HEAD_DIM // 2, 1)
        cos_t = jnp.concatenate([jnp.where(low, cos_p, cos_r), jnp.where(low, cos_r, cos_p)], axis=0)
        sin_t = jnp.concatenate([jnp.where(low, -sin_p, sin_r), jnp.where(low, -sin_r, sin_p)], axis=0)

        n_chunks = MIX_TM // CHUNK
        row = lax.broadcasted_iota(jnp.int32, (CHUNK, CHUNK), 0)
        col = lax.broadcasted_iota(jnp.int32, (CHUNK, CHUNK), 1)
        causal = row >= col
        rel = jnp.where(causal, row - col, 0).astype(F32)
        row_f = row.astype(F32)
        q_scale = HEAD_DIM ** -0.5
        heads_per_slab = MIX_SLAB // HEAD_DIM

        for p in range(RET_HEADS // heads_per_slab):
            q2 = project(p * MIX_SLAB)
            k2 = project(D_RET + p * MIX_SLAB)
            v2 = project(2 * D_RET + p * MIX_SLAB).astype(BF16)
            g2 = project(3 * D_RET + p * MIX_SLAB)
            for hh in range(heads_per_slab):
                h = p * heads_per_slab + hh
                lanes = slice(hh * HEAD_DIM, (hh + 1) * HEAD_DIM)
                lg = _log_gamma(h)
                decay_in = jnp.where(causal, jnp.exp(lg * rel), 0.0) * q_scale
                xi = jnp.exp(lg * (row_f + 1.0)) * q_scale
                zeta = jnp.exp(lg * (CHUNK - 1.0 - row_f))
                chunk_decay = math.exp(lg * CHUNK)
                gain = rg_ref[:, h * HEAD_DIM:(h + 1) * HEAD_DIM]

                q = q2[:, lanes]
                k = k2[:, lanes]
                v = v2[:, lanes]
                gate = g2[:, lanes]
                q = q * cos_t + pltpu.roll(q, HEAD_DIM // 2, 1) * sin_t
                k = k * cos_t + pltpu.roll(k, HEAD_DIM // 2, 1) * sin_t

                state = state_ref[h]
                for c in range(n_chunks):
                    rows = slice(c * CHUNK, (c + 1) * CHUNK)
                    qc, kc, vc = q[rows], k[rows], v[rows]
                    scores = lax.dot_general(qc.astype(BF16), kc.astype(BF16), (((1,), (1,)), ((), ())),
                                             preferred_element_type=F32) * decay_in
                    inner = jnp.dot(scores.astype(BF16), vc, preferred_element_type=F32)
                    cross = jnp.dot((qc * xi).astype(BF16), state.astype(BF16), preferred_element_type=F32)
                    kv = lax.dot_general((kc * zeta).astype(BF16), vc, (((0,), (0,)), ((), ())),
                                         preferred_element_type=F32)
                    state = state * chunk_decay + kv
                    ret = _layernorm_lanes(inner + cross) * gain
                    gc = gate[rows]
                    o_ref[rows, h * HEAD_DIM:(h + 1) * HEAD_DIM] = (ret * (gc * jax.nn.sigmoid(gc))).astype(BF16)
                state_ref[h] = state

        groups_per_slab = MIX_SLAB // GROUP_DIM
        for p in range(SGU_GROUPS // groups_per_slab):
            u2 = _gelu_tanh(project(u_base + p * MIX_SLAB))
            for gg in range(groups_per_slab):
                g = p * groups_per_slab + gg
                w = jnp.where(causal, ws_ref[g], 0.0).astype(BF16)
                bias = bs_ref[g]
                cols = slice(g * GROUP_DIM, (g + 1) * GROUP_DIM)
                for c in range(n_chunks):
                    rows = slice(c * CHUNK, (c + 1) * CHUNK)
                    mixed = jnp.dot(w, vsn_ref[rows, cols], preferred_element_type=F32) + bias
                    u = u2[rows, gg * GROUP_DIM:(gg + 1) * GROUP_DIM]
                    o_ref[rows, D_RET + g * GROUP_DIM: D_RET + (g + 1) * GROUP_DIM] = (u * mixed).astype(BF16)

        xn_ref[...] = xn_next_ref[...]


def _mix(x2d, positions, pre_g, w_in, ret_norm_g, sgu_norm_g, w_s, b_s):
    t = x2d.shape[0]
    seq = positions.shape[1]
    n_tiles = t // MIX_TM
    half = HEAD_DIM // 2
    inv_freq = ROPE_BASE ** (-jnp.arange(half, dtype=F32) / half)
    inv_freq = jnp.concatenate([inv_freq, inv_freq]).reshape(1, HEAD_DIM)
    w_slabs = w_in.reshape(D_MODEL, D_PROJ // MIX_SLAB, MIX_SLAB).transpose(1, 0, 2)

    def const(shape):
        return pl.BlockSpec(shape, lambda s: (0,) * len(shape))

    return pl.pallas_call(
        functools.partial(_mix_kernel, seq // MIX_TM),
        grid=(n_tiles + 1,),
        in_specs=[
            pl.BlockSpec((MIX_TM, D_MODEL), lambda s: (jnp.minimum(s, n_tiles - 1), 0)),
            pl.BlockSpec((MIX_TM, 1), lambda s: (jnp.maximum(s - 1, 0), 0)),
            const((1, D_MODEL)),
            pl.BlockSpec((D_PROJ // MIX_SLAB, D_MODEL, MIX_SLAB), lambda s: (0, 0, 0),
                         pipeline_mode=pl.Buffered(1)),
            const((1, HEAD_DIM)),
            const((1, D_RET)),
            const((1, D_SGU)),
            const((SGU_GROUPS, CHUNK, CHUNK)),
            const((SGU_GROUPS, CHUNK, 1)),
        ],
        out_specs=pl.BlockSpec((MIX_TM, D_MODEL), lambda s: (jnp.maximum(s - 1, 0), 0)),
        out_shape=jax.ShapeDtypeStruct((t, D_MODEL), BF16),
        scratch_shapes=[
            pltpu.VMEM((MIX_TM, D_MODEL), BF16),
            pltpu.VMEM((MIX_TM, D_MODEL), BF16),
            pltpu.VMEM((MIX_TM, LANES), F32),
            pltpu.VMEM((RET_HEADS, HEAD_DIM, HEAD_DIM), F32),
            pltpu.VMEM((MIX_TM, D_SGU), F32),
            pltpu.VMEM((MIX_TM, D_SGU), BF16),
        ],
        compiler_params=pltpu.CompilerParams(
            dimension_semantics=("arbitrary",),
            vmem_limit_bytes=VMEM_LIMIT_BYTES,
        ),
        name="token_mixing",
    )(x2d, positions.reshape(t, 1), pre_g, w_slabs, inv_freq, ret_norm_g, sgu_norm_g, w_s,
      b_s.reshape(SGU_GROUPS, CHUNK, 1))


def _out_proj_kernel(m_ref, w_ref, g_ref, x_ref, o_ref, y_ref, scale_ref):
    y_ref[...] = jnp.dot(m_ref[...], w_ref[...], preferred_element_type=F32)
    _store_row_scales(y_ref, scale_ref)
    for rb in range(OUT_TM // ROW_BLOCK):
        rows = slice(rb * ROW_BLOCK, (rb + 1) * ROW_BLOCK)
        scale = scale_ref[rows, :]
        for cb in range(D_MODEL // LANES):
            cols = slice(cb * LANES, (cb + 1) * LANES)
            o_ref[rows, cols] = x_ref[rows, cols] + (y_ref[rows, cols] * scale) * g_ref[:, cols]


def _out_proj(mix2d, w_out, g, x2d):
    t = x2d.shape[0]
    return pl.pallas_call(
        _out_proj_kernel,
        grid=(t // OUT_TM,),
        in_specs=[
            pl.BlockSpec((OUT_TM, D_MODEL), lambda i: (i, 0)),
            pl.BlockSpec((D_MODEL, D_MODEL), lambda i: (0, 0)),
            pl.BlockSpec((1, D_MODEL), lambda i: (0, 0)),
            pl.BlockSpec((OUT_TM, D_MODEL), lambda i: (i, 0)),
        ],
        out_specs=pl.BlockSpec((OUT_TM, D_MODEL), lambda i: (i, 0)),
        out_shape=jax.ShapeDtypeStruct((t, D_MODEL), F32),
        scratch_shapes=[pltpu.VMEM((OUT_TM, D_MODEL), F32), pltpu.VMEM((OUT_TM, LANES), F32)],
        compiler_params=pltpu.CompilerParams(
            dimension_semantics=("parallel",),
            vmem_limit_bytes=VMEM_LIMIT_BYTES,
        ),
        name="mix_out_proj",
    )(mix2d, w_out, g, x2d)


def kernel(x, positions, ffn1_pre_g, ffn1_w_gate, ffn1_w_up, ffn1_w_down, ffn1_post_g, mix_pre_g, w_in, ret_norm_g, sgu_norm_g, sgu_w_s, sgu_b_s, w_out, mix_post_g, ffn2_pre_g, ffn2_w_gate, ffn2_w_up, ffn2_w_down, ffn2_post_g):
    b, s, d = x.shape
    depth = ffn1_pre_g.shape[0]
    h = x.reshape(b * s, d)
    for l in range(depth):
        h = _ffn(h, ffn1_pre_g[l][None], ffn1_w_gate[l].astype(BF16), ffn1_w_up[l].astype(BF16),
                 ffn1_w_down[l].astype(BF16), ffn1_post_g[l][None])
        mixed = _mix(h, positions, mix_pre_g[l][None], w_in[l].astype(BF16), ret_norm_g[l][None],
                     sgu_norm_g[l][None], sgu_w_s[l], sgu_b_s[l])
        h = _out_proj(mixed, w_out[l].astype(BF16), mix_post_g[l][None], h)
        h = _ffn(h, ffn2_pre_g[l][None], ffn2_w_gate[l].astype(BF16), ffn2_w_up[l].astype(BF16),
                 ffn2_w_down[l].astype(BF16), ffn2_post_g[l][None])
    return h.reshape(b, s, d)
```

```python
import functools
import math

import jax
import jax.numpy as jnp
from jax import lax
from jax.experimental import pallas as pl
from jax.experimental.pallas import tpu as pltpu

D_MODEL = 2048
D_RET = D_MODEL // 2
D_SGU = D_MODEL - D_RET
RET_HEADS = 8
HEAD_DIM = D_RET // RET_HEADS
SGU_GROUPS = 8
GROUP_DIM = D_SGU // SGU_GROUPS
CHUNK = 128
D_FF = 5632
ROPE_BASE = 10000.0
EPS = 1e-6
D_PROJ = 4 * D_RET + 2 * D_SGU

F32 = jnp.float32
BF16 = jnp.bfloat16

VMEM_LIMIT_BYTES = 56 * 1024 * 1024

FFN_TM = 1024
FFN_TF = 512
FFN_OUT_ROWS = 256
MIX_TM = 512
MIX_SLAB = 512
OUT_TM = 512
LANES = 128
ROW_BLOCK = 128


def _rms_scale(v):
    return lax.rsqrt(jnp.mean(v * v, axis=-1, keepdims=True) + EPS)


def _store_row_scales(v_ref, scale_ref):
    scale_ref[...] = jnp.broadcast_to(_rms_scale(v_ref[...]), scale_ref.shape)


def _rmsnorm_rows_to_bf16(x_ref, g_ref, xn_ref, scale_ref):
    _store_row_scales(x_ref, scale_ref)
    for rb in range(x_ref.shape[0] // ROW_BLOCK):
        rows = slice(rb * ROW_BLOCK, (rb + 1) * ROW_BLOCK)
        scale = scale_ref[rows, :]
        for cb in range(x_ref.shape[1] // LANES):
            cols = slice(cb * LANES, (cb + 1) * LANES)
            xn_ref[rows, cols] = ((x_ref[rows, cols] * scale) * g_ref[:, cols]).astype(BF16)


def _ffn_kernel(x_ref, pre_g_ref, wg_ref, wu_ref, wd_ref, post_g_ref, o_hbm, xn_ref, acc_ref,
                scale_ref, stage_ref, out_sem):
    i = pl.program_id(0)
    j = pl.program_id(1)

    @pl.when(j == 0)
    def _():
        _rmsnorm_rows_to_bf16(x_ref, pre_g_ref, xn_ref, scale_ref)

    @pl.when((i == 0) & (j == 0))
    def _():
        acc_ref[...] = jnp.zeros_like(acc_ref)

    xn = xn_ref[...]
    gate = jnp.dot(xn, wg_ref[...], preferred_element_type=F32)
    up = jnp.dot(xn, wu_ref[...], preferred_element_type=F32)
    half_gate = 0.5 * gate
    gu = half_gate * up
    h = (gu + gu * jnp.tanh(half_gate)).astype(BF16)
    carried = jnp.where(j == 0, 0.0, acc_ref[...])
    acc_ref[...] = carried + jnp.dot(h, wd_ref[...], preferred_element_type=F32)

    @pl.when(j == pl.num_programs(1) - 1)
    def _():
        n_copies = FFN_TM // FFN_OUT_ROWS

        def out_copy(tile, c):
            rows = pl.ds(tile * FFN_TM + c * FFN_OUT_ROWS, FFN_OUT_ROWS)
            return pltpu.make_async_copy(stage_ref.at[c], o_hbm.at[rows, :], out_sem.at[c])

        @pl.when(i > 0)
        def _():
            for c in range(n_copies):
                out_copy(i - 1, c).wait()

        _store_row_scales(acc_ref, scale_ref)
        half_g = 0.5 * post_g_ref[...]
        for c in range(n_copies):
            for rb in range(FFN_OUT_ROWS // ROW_BLOCK):
                dst = slice(rb * ROW_BLOCK, (rb + 1) * ROW_BLOCK)
                rows = slice(c * FFN_OUT_ROWS + rb * ROW_BLOCK, c * FFN_OUT_ROWS + (rb + 1) * ROW_BLOCK)
                scale = scale_ref[rows, :]
                for cb in range(D_MODEL // LANES):
                    cols = slice(cb * LANES, (cb + 1) * LANES)
                    normed = (acc_ref[rows, cols] * scale) * half_g[:, cols]
                    stage_ref[c, dst, cols] = x_ref[rows, cols] + normed
            out_copy(i, c).start()

        @pl.when(i == pl.num_programs(0) - 1)
        def _():
            for c in range(n_copies):
                out_copy(i, c).wait()


def _ffn(x2d, pre_g, w_gate, w_up, w_down, post_g):
    t = x2d.shape[0]
    grid = (t // FFN_TM, D_FF // FFN_TF)
    return pl.pallas_call(
        _ffn_kernel,
        grid=grid,
        in_specs=[
            pl.BlockSpec((FFN_TM, D_MODEL), lambda i, j: (i, 0)),
            pl.BlockSpec((1, D_MODEL), lambda i, j: (0, 0)),
            pl.BlockSpec((D_MODEL, FFN_TF), lambda i, j: (0, j)),
            pl.BlockSpec((D_MODEL, FFN_TF), lambda i, j: (0, j)),
            pl.BlockSpec((FFN_TF, D_MODEL), lambda i, j: (j, 0)),
            pl.BlockSpec((1, D_MODEL), lambda i, j: (0, 0)),
        ],
        out_specs=pl.BlockSpec(memory_space=pl.ANY),
        out_shape=jax.ShapeDtypeStruct((t, D_MODEL), F32),
        scratch_shapes=[
            pltpu.VMEM((FFN_TM, D_MODEL), BF16),
            pltpu.VMEM((FFN_TM, D_MODEL), F32),
            pltpu.VMEM((FFN_TM, LANES), F32),
            pltpu.VMEM((FFN_TM // FFN_OUT_ROWS, FFN_OUT_ROWS, D_MODEL), F32),
            pltpu.SemaphoreType.DMA((FFN_TM // FFN_OUT_ROWS,)),
        ],
        compiler_params=pltpu.CompilerParams(
            dimension_semantics=("arbitrary", "arbitrary"),
            vmem_limit_bytes=VMEM_LIMIT_BYTES,
        ),
        name="ffn_half_step",
    )(x2d, pre_g, w_gate, w_up, w_down, post_g)


def _log_gamma(h):
    return math.log1p(-(2.0 ** (-5.0 - h)))


def _layernorm_lanes(v):
    mu = jnp.mean(v, axis=-1, keepdims=True)
    vc = v - mu
    return vc * lax.rsqrt(jnp.mean(vc * vc, axis=-1, keepdims=True) + EPS)


def _gelu_tanh(v):
    c = math.sqrt(2.0 / math.pi)
    neg2y = v * ((v * v) * (-2.0 * c * 0.044715) + (-2.0 * c))
    return v / (1.0 + jnp.exp(neg2y))


def _mix_kernel(tiles_per_row, x_ref, pos_ref, pre_g_ref, w_ref, invf_ref, rg_ref, sg_ref, ws_ref,
                bs_ref, o_ref, xn_ref, xn_next_ref, scale_ref, state_ref, vs_ref, vsn_ref):
    s = pl.program_id(0)

    @pl.when(s == 0)
    def _():
        _rmsnorm_rows_to_bf16(x_ref, pre_g_ref, xn_ref, scale_ref)

    @pl.when((s > 0) & ((s - 1) % tiles_per_row == 0))
    def _():
        state_ref[...] = jnp.zeros_like(state_ref)

    @pl.when(s > 0)
    def _():
        xn = xn_ref[...]

        def project(first_col):
            return jnp.dot(xn, w_ref[first_col // MIX_SLAB], preferred_element_type=F32)

        u_base = 4 * D_RET
        for p in range(D_SGU // MIX_SLAB):
            vs_ref[:, p * MIX_SLAB:(p + 1) * MIX_SLAB] = _gelu_tanh(project(u_base + D_SGU + p * MIX_SLAB))

        _rmsnorm_rows_to_bf16(x_ref, pre_g_ref, xn_next_ref, scale_ref)
        vsn_ref[...] = (_layernorm_lanes(vs_ref[...]) * sg_ref[...]).astype(BF16)

        lane = lax.broadcasted_iota(jnp.int32, (1, HEAD_DIM), 1)
        low = lane < HEAD_DIM // 2
        pos = pos_ref[...].astype(F32)
        ang = jnp.where(low, pos[:MIX_TM // 2], pos[MIX_TM // 2:]) * invf_ref[...]
        cos_p, sin_p = jnp.cos(ang), jnp.sin(ang)
        cos_r, sin_r = pltpu.roll(cos_p, HEAD_DIM // 2, 1), pltpu.roll(sin_p, HEAD_DIM // 2, 1)
        cos_t = jnp.concatenate([jnp.where(low, cos_p, cos_r), jnp.where(low, cos_r, cos_p)], axis=0)
        sin_t = jnp.concatenate([jnp.where(low, -sin_p, sin_r), jnp.where(low, -sin_r, sin_p)], axis=0)

        n_chunks = MIX_TM // CHUNK
        row = lax.broadcasted_iota(jnp.int32, (CHUNK, CHUNK), 0)
        col = lax.broadcasted_iota(jnp.int32, (CHUNK, CHUNK), 1)
        causal = row >= col
        rel = jnp.where(causal, row - col, 0).astype(F32)
        row_f = row.astype(F32)
        q_scale = HEAD_DIM ** -0.5
        heads_per_slab = MIX_SLAB // HEAD_DIM

        for p in range(RET_HEADS // heads_per_slab):
            q2 = project(p * MIX_SLAB)
            k2 = project(D_RET + p * MIX_SLAB)
            v2 = project(2 * D_RET + p * MIX_SLAB).astype(BF16)
            g2 = project(3 * D_RET + p * MIX_SLAB)
            for hh in range(heads_per_slab):
                h = p * heads_per_slab + hh
                lanes = slice(hh * HEAD_DIM, (hh + 1) * HEAD_DIM)
                lg = _log_gamma(h)
                decay_in = jnp.where(causal, jnp.exp(lg * rel), 0.0) * q_scale
                xi = jnp.exp(lg * (row_f + 1.0)) * q_scale
                zeta = jnp.exp(lg * (CHUNK - 1.0 - row_f))
                chunk_decay = math.exp(lg * CHUNK)
                gain = rg_ref[:, h * HEAD_DIM:(h + 1) * HEAD_DIM]

                q = q2[:, lanes]
                k = k2[:, lanes]
                v = v2[:, lanes]
                gate = g2[:, lanes]
                q = q * cos_t + pltpu.roll(q, HEAD_DIM // 2, 1) * sin_t
                k = k * cos_t + pltpu.roll(k, HEAD_DIM // 2, 1) * sin_t

                state = state_ref[h]
                for c in range(n_chunks):
                    rows = slice(c * CHUNK, (c + 1) * CHUNK)
                    qc, kc, vc = q[rows], k[rows], v[rows]
                    scores = lax.dot_general(qc.astype(BF16), kc.astype(BF16), (((1,), (1,)), ((), ())),
                                             preferred_element_type=F32) * decay_in
                    inner = jnp.dot(scores.astype(BF16), vc, preferred_element_type=F32)
                    cross = jnp.dot((qc * xi).astype(BF16), state.astype(BF16), preferred_element_type=F32)
                    kv = lax.dot_general((kc * zeta).astype(BF16), vc, (((0,), (0,)), ((), ())),
                                         preferred_element_type=F32)
                    state = state * chunk_decay + kv
                    ret = _layernorm_lanes(inner + cross) * gain
                    gc = gate[rows]
                    o_ref[rows, h * HEAD_DIM:(h + 1) * HEAD_DIM] = (ret * (gc * jax.nn.sigmoid(gc))).astype(BF16)
                state_ref[h] = state

        groups_per_slab = MIX_SLAB // GROUP_DIM
        for p in range(SGU_GROUPS // groups_per_slab):
            u2 = _gelu_tanh(project(u_base + p * MIX_SLAB))
            for gg in range(groups_per_slab):
                g = p * groups_per_slab + gg
                w = jnp.where(causal, ws_ref[g], 0.0).astype(BF16)
                bias = bs_ref[g]
                cols = slice(g * GROUP_DIM, (g + 1) * GROUP_DIM)
                for c in range(n_chunks):
                    rows = slice(c * CHUNK, (c + 1) * CHUNK)
                    mixed = jnp.dot(w, vsn_ref[rows, cols], preferred_element_type=F32) + bias
                    u = u2[rows, gg * GROUP_DIM:(gg + 1) * GROUP_DIM]
                    o_ref[rows, D_RET + g * GROUP_DIM: D_RET + (g + 1) * GROUP_DIM] = (u * mixed).astype(BF16)

        xn_ref[...] = xn_next_ref[...]


def _mix(x2d, positions, pre_g, w_in, ret_norm_g, sgu_norm_g, w_s, b_s):
    t = x2d.shape[0]
    seq = positions.shape[1]
    n_tiles = t // MIX_TM
    half = HEAD_DIM // 2
    inv_freq = ROPE_BASE ** (-jnp.arange(half, dtype=F32) / half)
    inv_freq = jnp.concatenate([inv_freq, inv_freq]).reshape(1, HEAD_DIM)
    n_slabs = D_PROJ // MIX_SLAB
    w_slabs = jnp.stack([w_in[:, k * MIX_SLAB:(k + 1) * MIX_SLAB] for k in range(n_slabs)]).astype(BF16)

    def const(shape):
        return pl.BlockSpec(shape, lambda s: (0,) * len(shape))

    return pl.pallas_call(
        functools.partial(_mix_kernel, seq // MIX_TM),
        grid=(n_tiles + 1,),
        in_specs=[
            pl.BlockSpec((MIX_TM, D_MODEL), lambda s: (jnp.minimum(s, n_tiles - 1), 0)),
            pl.BlockSpec((MIX_TM, 1), lambda s: (jnp.maximum(s - 1, 0), 0)),
            const((1, D_MODEL)),
            pl.BlockSpec((D_PROJ // MIX_SLAB, D_MODEL, MIX_SLAB), lambda s: (0, 0, 0),
                         pipeline_mode=pl.Buffered(1)),
            const((1, HEAD_DIM)),
            const((1, D_RET)),
            const((1, D_SGU)),
            const((SGU_GROUPS, CHUNK, CHUNK)),
            const((SGU_GROUPS, CHUNK, 1)),
        ],
        out_specs=pl.BlockSpec((MIX_TM, D_MODEL), lambda s: (jnp.maximum(s - 1, 0), 0)),
        out_shape=jax.ShapeDtypeStruct((t, D_MODEL), BF16),
        scratch_shapes=[
            pltpu.VMEM((MIX_TM, D_MODEL), BF16),
            pltpu.VMEM((MIX_TM, D_MODEL), BF16),
            pltpu.VMEM((MIX_TM, LANES), F32),
            pltpu.VMEM((RET_HEADS, HEAD_DIM, HEAD_DIM), F32),
            pltpu.VMEM((MIX_TM, D_SGU), F32),
            pltpu.VMEM((MIX_TM, D_SGU), BF16),
        ],
        compiler_params=pltpu.CompilerParams(
            dimension_semantics=("arbitrary",),
            vmem_limit_bytes=VMEM_LIMIT_BYTES,
        ),
        name="token_mixing",
    )(x2d, positions.reshape(t, 1), pre_g, w_slabs, inv_freq, ret_norm_g, sgu_norm_g, w_s,
      b_s.reshape(SGU_GROUPS, CHUNK, 1))


def _out_proj_kernel(m_ref, w_ref, g_ref, x_ref, o_ref, y_ref, scale_ref):
    y_ref[...] = jnp.dot(m_ref[...], w_ref[...], preferred_element_type=F32)
    _store_row_scales(y_ref, scale_ref)
    for rb in range(OUT_TM // ROW_BLOCK):
        rows = slice(rb * ROW_BLOCK, (rb + 1) * ROW_BLOCK)
        scale = scale_ref[rows, :]
        for cb in range(D_MODEL // LANES):
            cols = slice(cb * LANES, (cb + 1) * LANES)
            o_ref[rows, cols] = x_ref[rows, cols] + (y_ref[rows, cols] * scale) * g_ref[:, cols]


def _out_proj(mix2d, w_out, g, x2d):
    t = x2d.shape[0]
    return pl.pallas_call(
        _out_proj_kernel,
        grid=(t // OUT_TM,),
        in_specs=[
            pl.BlockSpec((OUT_TM, D_MODEL), lambda i: (i, 0)),
            pl.BlockSpec((D_MODEL, D_MODEL), lambda i: (0, 0)),
            pl.BlockSpec((1, D_MODEL), lambda i: (0, 0)),
            pl.BlockSpec((OUT_TM, D_MODEL), lambda i: (i, 0)),
        ],
        out_specs=pl.BlockSpec((OUT_TM, D_MODEL), lambda i: (i, 0)),
        out_shape=jax.ShapeDtypeStruct((t, D_MODEL), F32),
        scratch_shapes=[pltpu.VMEM((OUT_TM, D_MODEL), F32), pltpu.VMEM((OUT_TM, LANES), F32)],
        compiler_params=pltpu.CompilerParams(
            dimension_semantics=("parallel",),
            vmem_limit_bytes=VMEM_LIMIT_BYTES,
        ),
        name="mix_out_proj",
    )(mix2d, w_out, g, x2d)


def kernel(x, positions, ffn1_pre_g, ffn1_w_gate, ffn1_w_up, ffn1_w_down, ffn1_post_g, mix_pre_g, w_in, ret_norm_g, sgu_norm_g, sgu_w_s, sgu_b_s, w_out, mix_post_g, ffn2_pre_g, ffn2_w_gate, ffn2_w_up, ffn2_w_down, ffn2_post_g):
    b, s, d = x.shape
    depth = ffn1_pre_g.shape[0]
    h = x.reshape(b * s, d)
    for l in range(depth):
        h = _ffn(h, ffn1_pre_g[l][None], ffn1_w_gate[l].astype(BF16), ffn1_w_up[l].astype(BF16),
                 ffn1_w_down[l].astype(BF16), ffn1_post_g[l][None])
        mixed = _mix(h, positions, mix_pre_g[l][None], w_in[l], ret_norm_g[l][None],
                     sgu_norm_g[l][None], sgu_w_s[l], sgu_b_s[l])
        h = _out_proj(mixed, w_out[l].astype(BF16), mix_post_g[l][None], h)
        h = _ffn(h, ffn2_pre_g[l][None], ffn2_w_gate[l].astype(BF16), ffn2_w_up[l].astype(BF16),
                 ffn2_w_down[l].astype(BF16), ffn2_post_g[l][None])
    return h.reshape(b, s, d)
```

```python
import functools
import math

import jax
import jax.numpy as jnp
from jax import lax
from jax.experimental import pallas as pl
from jax.experimental.pallas import tpu as pltpu

D_MODEL = 2048
D_RET = D_MODEL // 2
D_SGU = D_MODEL - D_RET
RET_HEADS = 8
HEAD_DIM = D_RET // RET_HEADS
SGU_GROUPS = 8
GROUP_DIM = D_SGU // SGU_GROUPS
CHUNK = 128
D_FF = 5632
ROPE_BASE = 10000.0
EPS = 1e-6
D_PROJ = 4 * D_RET + 2 * D_SGU

F32 = jnp.float32
BF16 = jnp.bfloat16

VMEM_LIMIT_BYTES = 56 * 1024 * 1024

FFN_TM = 1024
FFN_TF = 512
FFN_OUT_ROWS = 256
MIX_TM = 512
MIX_SLAB = 512
OUT_TM = 512
LANES = 128
ROW_BLOCK = 128


def _rms_scale(v):
    return lax.rsqrt(jnp.mean(v * v, axis=-1, keepdims=True) + EPS)


def _store_row_scales(v_ref, scale_ref):
    scale_ref[...] = jnp.broadcast_to(_rms_scale(v_ref[...]), scale_ref.shape)


def _rmsnorm_rows_to_bf16(x_ref, g_ref, xn_ref, scale_ref):
    _store_row_scales(x_ref, scale_ref)
    for rb in range(x_ref.shape[0] // ROW_BLOCK):
        rows = slice(rb * ROW_BLOCK, (rb + 1) * ROW_BLOCK)
        scale = scale_ref[rows, :]
        for cb in range(x_ref.shape[1] // LANES):
            cols = slice(cb * LANES, (cb + 1) * LANES)
            xn_ref[rows, cols] = ((x_ref[rows, cols] * scale) * g_ref[:, cols]).astype(BF16)


def _ffn_kernel(x_ref, pre_g_ref, wg_ref, wu_ref, wd_ref, post_g_ref, o_hbm, xn_ref, acc_ref,
                scale_ref, stage_ref, out_sem):
    i = pl.program_id(0)
    j = pl.program_id(1)

    @pl.when(j == 0)
    def _():
        _rmsnorm_rows_to_bf16(x_ref, pre_g_ref, xn_ref, scale_ref)

    @pl.when((i == 0) & (j == 0))
    def _():
        acc_ref[...] = jnp.zeros_like(acc_ref)

    xn = xn_ref[...]
    gate = jnp.dot(xn, wg_ref[...], preferred_element_type=F32)
    up = jnp.dot(xn, wu_ref[...], preferred_element_type=F32)
    half_gate = 0.5 * gate
    gu = half_gate * up
    h = (gu + gu * jnp.tanh(half_gate)).astype(BF16)
    carried = jnp.where(j == 0, 0.0, acc_ref[...])
    acc_ref[...] = carried + jnp.dot(h, wd_ref[...], preferred_element_type=F32)

    @pl.when(j == pl.num_programs(1) - 1)
    def _():
        n_copies = FFN_TM // FFN_OUT_ROWS

        def out_copy(tile, c):
            rows = pl.ds(tile * FFN_TM + c * FFN_OUT_ROWS, FFN_OUT_ROWS)
            return pltpu.make_async_copy(stage_ref.at[c], o_hbm.at[rows, :], out_sem.at[c])

        @pl.when(i > 0)
        def _():
            for c in range(n_copies):
                out_copy(i - 1, c).wait()

        _store_row_scales(acc_ref, scale_ref)
        half_g = 0.5 * post_g_ref[...]
        for c in range(n_copies):
            for rb in range(FFN_OUT_ROWS // ROW_BLOCK):
                dst = slice(rb * ROW_BLOCK, (rb + 1) * ROW_BLOCK)
                rows = slice(c * FFN_OUT_ROWS + rb * ROW_BLOCK, c * FFN_OUT_ROWS + (rb + 1) * ROW_BLOCK)
                scale = scale_ref[rows, :]
                for cb in range(D_MODEL // LANES):
                    cols = slice(cb * LANES, (cb + 1) * LANES)
                    normed = (acc_ref[rows, cols] * scale) * half_g[:, cols]
                    stage_ref[c, dst, cols] = x_ref[rows, cols] + normed
            out_copy(i, c).start()

        @pl.when(i == pl.num_programs(0) - 1)
        def _():
            for c in range(n_copies):
                out_copy(i, c).wait()


def _ffn(x2d, pre_g, w_gate, w_up, w_down, post_g):
    t = x2d.shape[0]
    grid = (t // FFN_TM, D_FF // FFN_TF)
    return pl.pallas_call(
        _ffn_kernel,
        grid=grid,
        in_specs=[
            pl.BlockSpec((FFN_TM, D_MODEL), lambda i, j: (i, 0)),
            pl.BlockSpec((1, D_MODEL), lambda i, j: (0, 0)),
            pl.BlockSpec((D_MODEL, FFN_TF), lambda i, j: (0, j)),
            pl.BlockSpec((D_MODEL, FFN_TF), lambda i, j: (0, j)),
            pl.BlockSpec((FFN_TF, D_MODEL), lambda i, j: (j, 0)),
            pl.BlockSpec((1, D_MODEL), lambda i, j: (0, 0)),
        ],
        out_specs=pl.BlockSpec(memory_space=pl.ANY),
        out_shape=jax.ShapeDtypeStruct((t, D_MODEL), F32),
        scratch_shapes=[
            pltpu.VMEM((FFN_TM, D_MODEL), BF16),
            pltpu.VMEM((FFN_TM, D_MODEL), F32),
            pltpu.VMEM((FFN_TM, LANES), F32),
            pltpu.VMEM((FFN_TM // FFN_OUT_ROWS, FFN_OUT_ROWS, D_MODEL), F32),
            pltpu.SemaphoreType.DMA((FFN_TM // FFN_OUT_ROWS,)),
        ],
        compiler_params=pltpu.CompilerParams(
            dimension_semantics=("arbitrary", "arbitrary"),
            vmem_limit_bytes=VMEM_LIMIT_BYTES,
        ),
        name="ffn_half_step",
    )(x2d, pre_g, w_gate, w_up, w_down, post_g)


def _log_gamma(h):
    return math.log1p(-(2.0 ** (-5.0 - h)))


def _layernorm_lanes(v):
    mu = jnp.mean(v, axis=-1, keepdims=True)
    vc = v - mu
    return vc * lax.rsqrt(jnp.mean(vc * vc, axis=-1, keepdims=True) + EPS)


def _gelu_tanh(v):
    c = math.sqrt(2.0 / math.pi)
    neg2y = v * ((v * v) * (-2.0 * c * 0.044715) + (-2.0 * c))
    return v / (1.0 + jnp.exp(neg2y))


def _mix_kernel(tiles_per_row, x_ref, pos_ref, pre_g_ref, w_ref, invf_ref, rg_ref, sg_ref, ws_ref,
                bs_ref, o_ref, xn_ref, xn_next_ref, scale_ref, state_ref, vs_ref, vsn_ref, decay_ref,
                wtril_ref):
    s = pl.program_id(0)

    @pl.when(s == 0)
    def _():
        _rmsnorm_rows_to_bf16(x_ref, pre_g_ref, xn_ref, scale_ref)
        row = lax.broadcasted_iota(jnp.int32, (CHUNK, CHUNK), 0)
        col = lax.broadcasted_iota(jnp.int32, (CHUNK, CHUNK), 1)
        causal = row >= col
        rel = jnp.where(causal, row - col, 0).astype(F32)
        row_f = row.astype(F32)
        q_scale = HEAD_DIM ** -0.5
        for h in range(RET_HEADS):
            lg = _log_gamma(h)
            decay_ref[0, h] = jnp.where(causal, jnp.exp(lg * rel), 0.0) * q_scale
            decay_ref[1, h] = jnp.exp(lg * (row_f + 1.0)) * q_scale
            decay_ref[2, h] = jnp.exp(lg * (CHUNK - 1.0 - row_f))
        for g in range(SGU_GROUPS):
            wtril_ref[g] = jnp.where(causal, ws_ref[g], 0.0).astype(BF16)

    @pl.when((s > 0) & ((s - 1) % tiles_per_row == 0))
    def _():
        state_ref[...] = jnp.zeros_like(state_ref)

    @pl.when(s > 0)
    def _():
        def project(first_col):
            return jnp.dot(xn_ref[...], w_ref[first_col // MIX_SLAB], preferred_element_type=F32)

        _rmsnorm_rows_to_bf16(x_ref, pre_g_ref, xn_next_ref, scale_ref)

        u_base = 4 * D_RET
        for p in range(D_SGU // MIX_SLAB):
            vs_ref[:, p * MIX_SLAB:(p + 1) * MIX_SLAB] = _gelu_tanh(project(u_base + D_SGU + p * MIX_SLAB))
        vsn_ref[...] = (_layernorm_lanes(vs_ref[...]) * sg_ref[...]).astype(BF16)

        lane = lax.broadcasted_iota(jnp.int32, (1, HEAD_DIM), 1)
        low = lane < HEAD_DIM // 2
        pos = pos_ref[...].astype(F32)
        ang = jnp.where(low, pos[:MIX_TM // 2], pos[MIX_TM // 2:]) * invf_ref[...]
        cos_p, sin_p = jnp.cos(ang), jnp.sin(ang)
        cos_r, sin_r = pltpu.roll(cos_p, HEAD_DIM // 2, 1), pltpu.roll(sin_p, HEAD_DIM // 2, 1)
        cos_t = jnp.concatenate([jnp.where(low, cos_p, cos_r), jnp.where(low, cos_r, cos_p)], axis=0)
        sin_t = jnp.concatenate([jnp.where(low, -sin_p, sin_r), jnp.where(low, -sin_r, sin_p)], axis=0)

        n_chunks = MIX_TM // CHUNK
        heads_per_slab = MIX_SLAB // HEAD_DIM

        for p in range(RET_HEADS // heads_per_slab):
            q2 = project(p * MIX_SLAB)
            k2 = project(D_RET + p * MIX_SLAB)
            v2 = project(2 * D_RET + p * MIX_SLAB).astype(BF16)
            g2 = project(3 * D_RET + p * MIX_SLAB)
            for hh in range(heads_per_slab):
                h = p * heads_per_slab + hh
                lanes = slice(hh * HEAD_DIM, (hh + 1) * HEAD_DIM)
                decay_in, xi, zeta = decay_ref[0, h], decay_ref[1, h], decay_ref[2, h]
                chunk_decay = math.exp(_log_gamma(h) * CHUNK)
                gain = rg_ref[:, h * HEAD_DIM:(h + 1) * HEAD_DIM]

                q = q2[:, lanes]
                k = k2[:, lanes]
                v = v2[:, lanes]
                gate = g2[:, lanes]
                q = q * cos_t + pltpu.roll(q, HEAD_DIM // 2, 1) * sin_t
                k = k * cos_t + pltpu.roll(k, HEAD_DIM // 2, 1) * sin_t

                state = state_ref[h]
                for c in range(n_chunks):
                    rows = slice(c * CHUNK, (c + 1) * CHUNK)
                    qc, kc, vc = q[rows], k[rows], v[rows]
                    scores = lax.dot_general(qc.astype(BF16), kc.astype(BF16), (((1,), (1,)), ((), ())),
                                             preferred_element_type=F32) * decay_in
                    inner = jnp.dot(scores.astype(BF16), vc, preferred_element_type=F32)
                    cross = jnp.dot((qc * xi).astype(BF16), state.astype(BF16), preferred_element_type=F32)
                    kv = lax.dot_general((kc * zeta).astype(BF16), vc, (((0,), (0,)), ((), ())),
                                         preferred_element_type=F32)
                    state = state * chunk_decay + kv
                    ret = _layernorm_lanes(inner + cross) * gain
                    rg = ret * (0.5 * gate[rows])
                    o_ref[rows, h * HEAD_DIM:(h + 1) * HEAD_DIM] = (
                        rg + rg * jnp.tanh(0.5 * gate[rows])).astype(BF16)
                state_ref[h] = state

        groups_per_slab = MIX_SLAB // GROUP_DIM
        for p in range(SGU_GROUPS // groups_per_slab):
            u2 = _gelu_tanh(project(u_base + p * MIX_SLAB))
            for gg in range(groups_per_slab):
                g = p * groups_per_slab + gg
                w = wtril_ref[g]
                bias = bs_ref[g]
                cols = slice(g * GROUP_DIM, (g + 1) * GROUP_DIM)
                for c in range(n_chunks):
                    rows = slice(c * CHUNK, (c + 1) * CHUNK)
                    mixed = jnp.dot(w, vsn_ref[rows, cols], preferred_element_type=F32) + bias
                    u = u2[rows, gg * GROUP_DIM:(gg + 1) * GROUP_DIM]
                    o_ref[rows, D_RET + g * GROUP_DIM: D_RET + (g + 1) * GROUP_DIM] = (u * mixed).astype(BF16)

        xn_ref[...] = xn_next_ref[...]


def _mix(x2d, positions, pre_g, w_in, ret_norm_g, sgu_norm_g, w_s, b_s):
    t = x2d.shape[0]
    seq = positions.shape[1]
    n_tiles = t // MIX_TM
    half = HEAD_DIM // 2
    inv_freq = ROPE_BASE ** (-jnp.arange(half, dtype=F32) / half)
    inv_freq = jnp.concatenate([inv_freq, inv_freq]).reshape(1, HEAD_DIM)
    n_slabs = D_PROJ // MIX_SLAB
    w_slabs = jnp.stack([w_in[:, k * MIX_SLAB:(k + 1) * MIX_SLAB] for k in range(n_slabs)]).astype(BF16)

    def const(shape):
        return pl.BlockSpec(shape, lambda s: (0,) * len(shape))

    return pl.pallas_call(
        functools.partial(_mix_kernel, seq // MIX_TM),
        grid=(n_tiles + 1,),
        in_specs=[
            pl.BlockSpec((MIX_TM, D_MODEL), lambda s: (jnp.minimum(s, n_tiles - 1), 0)),
            pl.BlockSpec((MIX_TM, 1), lambda s: (jnp.maximum(s - 1, 0), 0)),
            const((1, D_MODEL)),
            pl.BlockSpec((D_PROJ // MIX_SLAB, D_MODEL, MIX_SLAB), lambda s: (0, 0, 0),
                         pipeline_mode=pl.Buffered(1)),
            const((1, HEAD_DIM)),
            const((1, D_RET)),
            const((1, D_SGU)),
            const((SGU_GROUPS, CHUNK, CHUNK)),
            const((SGU_GROUPS, CHUNK, 1)),
        ],
        out_specs=pl.BlockSpec((MIX_TM, D_MODEL), lambda s: (jnp.maximum(s - 1, 0), 0)),
        out_shape=jax.ShapeDtypeStruct((t, D_MODEL), BF16),
        scratch_shapes=[
            pltpu.VMEM((MIX_TM, D_MODEL), BF16),
            pltpu.VMEM((MIX_TM, D_MODEL), BF16),
            pltpu.VMEM((MIX_TM, LANES), F32),
            pltpu.VMEM((RET_HEADS, HEAD_DIM, HEAD_DIM), F32),
            pltpu.VMEM((MIX_TM, D_SGU), F32),
            pltpu.VMEM((MIX_TM, D_SGU), BF16),
            pltpu.VMEM((3, RET_HEADS, CHUNK, CHUNK), F32),
            pltpu.VMEM((SGU_GROUPS, CHUNK, CHUNK), BF16),
        ],
        compiler_params=pltpu.CompilerParams(
            dimension_semantics=("arbitrary",),
            vmem_limit_bytes=VMEM_LIMIT_BYTES,
        ),
        name="token_mixing",
    )(x2d, positions.reshape(t, 1), pre_g, w_slabs, inv_freq, ret_norm_g, sgu_norm_g, w_s,
      b_s.reshape(SGU_GROUPS, CHUNK, 1))


def _out_proj_kernel(m_ref, w_ref, g_ref, x_ref, o_ref, y_ref, scale_ref):
    y_ref[...] = jnp.dot(m_ref[...], w_ref[...], preferred_element_type=F32)
    _store_row_scales(y_ref, scale_ref)
    for rb in range(OUT_TM // ROW_BLOCK):
        rows = slice(rb * ROW_BLOCK, (rb + 1) * ROW_BLOCK)
        scale = scale_ref[rows, :]
        for cb in range(D_MODEL // LANES):
            cols = slice(cb * LANES, (cb + 1) * LANES)
            o_ref[rows, cols] = x_ref[rows, cols] + (y_ref[rows, cols] * scale) * g_ref[:, cols]


def _out_proj(mix2d, w_out, g, x2d):
    t = x2d.shape[0]
    return pl.pallas_call(
        _out_proj_kernel,
        grid=(t // OUT_TM,),
        in_specs=[
            pl.BlockSpec((OUT_TM, D_MODEL), lambda i: (i, 0)),
            pl.BlockSpec((D_MODEL, D_MODEL), lambda i: (0, 0)),
            pl.BlockSpec((1, D_MODEL), lambda i: (0, 0)),
            pl.BlockSpec((OUT_TM, D_MODEL), lambda i: (i, 0)),
        ],
        out_specs=pl.BlockSpec((OUT_TM, D_MODEL), lambda i: (i, 0)),
        out_shape=jax.ShapeDtypeStruct((t, D_MODEL), F32),
        scratch_shapes=[pltpu.VMEM((OUT_TM, D_MODEL), F32), pltpu.VMEM((OUT_TM, LANES), F32)],
        compiler_params=pltpu.CompilerParams(
            dimension_semantics=("parallel",),
            vmem_limit_bytes=VMEM_LIMIT_BYTES,
        ),
        name="mix_out_proj",
    )(mix2d, w_out, g, x2d)


def kernel(x, positions, ffn1_pre_g, ffn1_w_gate, ffn1_w_up, ffn1_w_down, ffn1_post_g, mix_pre_g, w_in, ret_norm_g, sgu_norm_g, sgu_w_s, sgu_b_s, w_out, mix_post_g, ffn2_pre_g, ffn2_w_gate, ffn2_w_up, ffn2_w_down, ffn2_post_g):
    b, s, d = x.shape
    depth = ffn1_pre_g.shape[0]
    h = x.reshape(b * s, d)
    for l in range(depth):
        h = _ffn(h, ffn1_pre_g[l][None], ffn1_w_gate[l].astype(BF16), ffn1_w_up[l].astype(BF16),
                 ffn1_w_down[l].astype(BF16), ffn1_post_g[l][None])
        mixed = _mix(h, positions, mix_pre_g[l][None], w_in[l], ret_norm_g[l][None],
                     sgu_norm_g[l][None], sgu_w_s[l], sgu_b_s[l])
        h = _out_proj(mixed, w_out[l].astype(BF16), mix_post_g[l][None], h)
        h = _ffn(h, ffn2_pre_g[l][None], ffn2_w_gate[l].astype(BF16), ffn2_w_up[l].astype(BF16),
                 ffn2_w_down[l].astype(BF16), ffn2_post_g[l][None])
    return h.reshape(b, s, d)
```

```python
import functools
import math

import jax
import jax.numpy as jnp
from jax import lax
from jax.experimental import pallas as pl
from jax.experimental.pallas import tpu as pltpu

D_MODEL = 2048
D_RET = D_MODEL // 2
D_SGU = D_MODEL - D_RET
RET_HEADS = 8
HEAD_DIM = D_RET // RET_HEADS
SGU_GROUPS = 8
GROUP_DIM = D_SGU // SGU_GROUPS
CHUNK = 128
D_FF = 5632
ROPE_BASE = 10000.0
EPS = 1e-6
D_PROJ = 4 * D_RET + 2 * D_SGU

F32 = jnp.float32
BF16 = jnp.bfloat16

VMEM_LIMIT_BYTES = 56 * 1024 * 1024

FFN_TM = 1024
FFN_TF = 512
FFN_OUT_ROWS = 256
MIX_TM = 512
MIX_SLAB = 512
OUT_TM = 512
LANES = 128
ROW_BLOCK = 128


def _rms_scale(v):
    return lax.rsqrt(jnp.mean(v * v, axis=-1, keepdims=True) + EPS)


def _store_row_scales(v_ref, scale_ref):
    scale_ref[...] = jnp.broadcast_to(_rms_scale(v_ref[...]), scale_ref.shape)


def _rmsnorm_rows_to_bf16(x_ref, g_ref, xn_ref, scale_ref):
    _store_row_scales(x_ref, scale_ref)
    for rb in range(x_ref.shape[0] // ROW_BLOCK):
        rows = slice(rb * ROW_BLOCK, (rb + 1) * ROW_BLOCK)
        scale = scale_ref[rows, :]
        for cb in range(x_ref.shape[1] // LANES):
            cols = slice(cb * LANES, (cb + 1) * LANES)
            xn_ref[rows, cols] = ((x_ref[rows, cols] * scale) * g_ref[:, cols]).astype(BF16)


def _ffn_kernel(x_ref, pre_g_ref, wgu_ref, wd_ref, post_g_ref, o_hbm, xn_ref, acc_ref,
                scale_ref, stage_ref, out_sem):
    i = pl.program_id(0)
    j = pl.program_id(1)

    @pl.when(j == 0)
    def _():
        _rmsnorm_rows_to_bf16(x_ref, pre_g_ref, xn_ref, scale_ref)

    @pl.when((i == 0) & (j == 0))
    def _():
        acc_ref[...] = jnp.zeros_like(acc_ref)

    xn = xn_ref[...]
    gate_up = jnp.dot(xn, wgu_ref[...], preferred_element_type=F32)
    gate, up = gate_up[:, :FFN_TF], gate_up[:, FFN_TF:]
    half_gate = 0.5 * gate
    gu = half_gate * up
    h = (gu + gu * jnp.tanh(half_gate)).astype(BF16)
    carried = jnp.where(j == 0, 0.0, acc_ref[...])
    acc_ref[...] = carried + jnp.dot(h, wd_ref[...], preferred_element_type=F32)

    @pl.when(j == pl.num_programs(1) - 1)
    def _():
        n_copies = FFN_TM // FFN_OUT_ROWS

        def out_copy(tile, c):
            rows = pl.ds(tile * FFN_TM + c * FFN_OUT_ROWS, FFN_OUT_ROWS)
            return pltpu.make_async_copy(stage_ref.at[c], o_hbm.at[rows, :], out_sem.at[c])

        @pl.when(i > 0)
        def _():
            for c in range(n_copies):
                out_copy(i - 1, c).wait()

        _store_row_scales(acc_ref, scale_ref)
        half_g = 0.5 * post_g_ref[...]
        for c in range(n_copies):
            for rb in range(FFN_OUT_ROWS // ROW_BLOCK):
                dst = slice(rb * ROW_BLOCK, (rb + 1) * ROW_BLOCK)
                rows = slice(c * FFN_OUT_ROWS + rb * ROW_BLOCK, c * FFN_OUT_ROWS + (rb + 1) * ROW_BLOCK)
                scale = scale_ref[rows, :]
                for cb in range(D_MODEL // LANES):
                    cols = slice(cb * LANES, (cb + 1) * LANES)
                    normed = (acc_ref[rows, cols] * scale) * half_g[:, cols]
                    stage_ref[c, dst, cols] = x_ref[rows, cols] + normed
            out_copy(i, c).start()

        @pl.when(i == pl.num_programs(0) - 1)
        def _():
            for c in range(n_copies):
                out_copy(i, c).wait()


def _ffn(x2d, pre_g, w_gate, w_up, w_down, post_g):
    t = x2d.shape[0]
    n_blocks = D_FF // FFN_TF
    grid = (t // FFN_TM, n_blocks)
    w_gate_up = jnp.concatenate([w_gate.reshape(D_MODEL, n_blocks, FFN_TF),
                                 w_up.reshape(D_MODEL, n_blocks, FFN_TF)], axis=2)
    w_gate_up = w_gate_up.reshape(D_MODEL, 2 * D_FF).astype(BF16)
    return pl.pallas_call(
        _ffn_kernel,
        grid=grid,
        in_specs=[
            pl.BlockSpec((FFN_TM, D_MODEL), lambda i, j: (i, 0)),
            pl.BlockSpec((1, D_MODEL), lambda i, j: (0, 0)),
            pl.BlockSpec((D_MODEL, 2 * FFN_TF), lambda i, j: (0, j)),
            pl.BlockSpec((FFN_TF, D_MODEL), lambda i, j: (j, 0)),
            pl.BlockSpec((1, D_MODEL), lambda i, j: (0, 0)),
        ],
        out_specs=pl.BlockSpec(memory_space=pl.ANY),
        out_shape=jax.ShapeDtypeStruct((t, D_MODEL), F32),
        scratch_shapes=[
            pltpu.VMEM((FFN_TM, D_MODEL), BF16),
            pltpu.VMEM((FFN_TM, D_MODEL), F32),
            pltpu.VMEM((FFN_TM, LANES), F32),
            pltpu.VMEM((FFN_TM // FFN_OUT_ROWS, FFN_OUT_ROWS, D_MODEL), F32),
            pltpu.SemaphoreType.DMA((FFN_TM // FFN_OUT_ROWS,)),
        ],
        compiler_params=pltpu.CompilerParams(
            dimension_semantics=("arbitrary", "arbitrary"),
            vmem_limit_bytes=VMEM_LIMIT_BYTES,
        ),
        name="ffn_half_step",
    )(x2d, pre_g, w_gate_up, w_down.astype(BF16), post_g)


def _log_gamma(h):
    return math.log1p(-(2.0 ** (-5.0 - h)))


def _layernorm_lanes(v):
    mu = jnp.mean(v, axis=-1, keepdims=True)
    vc = v - mu
    return vc * lax.rsqrt(jnp.mean(vc * vc, axis=-1, keepdims=True) + EPS)


def _gelu_tanh(v):
    c = math.sqrt(2.0 / math.pi)
    neg2y = v * ((v * v) * (-2.0 * c * 0.044715) + (-2.0 * c))
    return v / (1.0 + jnp.exp(neg2y))


def _mix_kernel(tiles_per_row, x_ref, pos_ref, pre_g_ref, w_ref, invf_ref, rg_ref, sg_ref, ws_ref,
                bs_ref, o_ref, xn_ref, xn_next_ref, scale_ref, state_ref, vs_ref, vsn_ref, decay_ref,
                wtril_ref):
    s = pl.program_id(0)

    @pl.when(s == 0)
    def _():
        _rmsnorm_rows_to_bf16(x_ref, pre_g_ref, xn_ref, scale_ref)
        row = lax.broadcasted_iota(jnp.int32, (CHUNK, CHUNK), 0)
        col = lax.broadcasted_iota(jnp.int32, (CHUNK, CHUNK), 1)
        causal = row >= col
        rel = jnp.where(causal, row - col, 0).astype(F32)
        row_f = row.astype(F32)
        q_scale = HEAD_DIM ** -0.5
        for h in range(RET_HEADS):
            lg = _log_gamma(h)
            decay_ref[0, h] = jnp.where(causal, jnp.exp(lg * rel), 0.0) * q_scale
            decay_ref[1, h] = jnp.exp(lg * (row_f + 1.0)) * q_scale
            decay_ref[2, h] = jnp.exp(lg * (CHUNK - 1.0 - row_f))
        for g in range(SGU_GROUPS):
            wtril_ref[g] = jnp.where(causal, ws_ref[g], 0.0).astype(BF16)

    @pl.when((s > 0) & ((s - 1) % tiles_per_row == 0))
    def _():
        state_ref[...] = jnp.zeros_like(state_ref)

    @pl.when(s > 0)
    def _():
        def project(first_col):
            return jnp.dot(xn_ref[...], w_ref[first_col // MIX_SLAB], preferred_element_type=F32)

        _rmsnorm_rows_to_bf16(x_ref, pre_g_ref, xn_next_ref, scale_ref)

        u_base = 4 * D_RET
        for p in range(D_SGU // MIX_SLAB):
            vs_ref[:, p * MIX_SLAB:(p + 1) * MIX_SLAB] = _gelu_tanh(project(u_base + D_SGU + p * MIX_SLAB))
        vsn_ref[...] = (_layernorm_lanes(vs_ref[...]) * sg_ref[...]).astype(BF16)

        lane = lax.broadcasted_iota(jnp.int32, (1, HEAD_DIM), 1)
        low = lane < HEAD_DIM // 2
        pos = pos_ref[...].astype(F32)
        ang = jnp.where(low, pos[:MIX_TM // 2], pos[MIX_TM // 2:]) * invf_ref[...]
        cos_p, sin_p = jnp.cos(ang), jnp.sin(ang)
        cos_r, sin_r = pltpu.roll(cos_p, HEAD_DIM // 2, 1), pltpu.roll(sin_p, HEAD_DIM // 2, 1)
        cos_t = jnp.concatenate([jnp.where(low, cos_p, cos_r), jnp.where(low, cos_r, cos_p)], axis=0)
        sin_t = jnp.concatenate([jnp.where(low, -sin_p, sin_r), jnp.where(low, -sin_r, sin_p)], axis=0)

        n_chunks = MIX_TM // CHUNK
        heads_per_slab = MIX_SLAB // HEAD_DIM

        for p in range(RET_HEADS // heads_per_slab):
            q2 = project(p * MIX_SLAB)
            k2 = project(D_RET + p * MIX_SLAB)
            v2 = project(2 * D_RET + p * MIX_SLAB).astype(BF16)
            g2 = project(3 * D_RET + p * MIX_SLAB)
            for hh in range(heads_per_slab):
                h = p * heads_per_slab + hh
                lanes = slice(hh * HEAD_DIM, (hh + 1) * HEAD_DIM)
                decay_in, xi, zeta = decay_ref[0, h], decay_ref[1, h], decay_ref[2, h]
                chunk_decay = math.exp(_log_gamma(h) * CHUNK)
                gain = rg_ref[:, h * HEAD_DIM:(h + 1) * HEAD_DIM]

                q = q2[:, lanes]
                k = k2[:, lanes]
                v = v2[:, lanes]
                gate = g2[:, lanes]
                q = q * cos_t + pltpu.roll(q, HEAD_DIM // 2, 1) * sin_t
                k = k * cos_t + pltpu.roll(k, HEAD_DIM // 2, 1) * sin_t

                state = state_ref[h]
                for c in range(n_chunks):
                    rows = slice(c * CHUNK, (c + 1) * CHUNK)
                    qc, kc, vc = q[rows], k[rows], v[rows]
                    scores = lax.dot_general(qc.astype(BF16), kc.astype(BF16), (((1,), (1,)), ((), ())),
                                             preferred_element_type=F32) * decay_in
                    inner = jnp.dot(scores.astype(BF16), vc, preferred_element_type=F32)
                    cross = jnp.dot((qc * xi).astype(BF16), state.astype(BF16), preferred_element_type=F32)
                    kv = lax.dot_general((kc * zeta).astype(BF16), vc, (((0,), (0,)), ((), ())),
                                         preferred_element_type=F32)
                    state = state * chunk_decay + kv
                    ret = _layernorm_lanes(inner + cross) * gain
                    rg = ret * (0.5 * gate[rows])
                    o_ref[rows, h * HEAD_DIM:(h + 1) * HEAD_DIM] = (
                        rg + rg * jnp.tanh(0.5 * gate[rows])).astype(BF16)
                state_ref[h] = state

        groups_per_slab = MIX_SLAB // GROUP_DIM
        for p in range(SGU_GROUPS // groups_per_slab):
            u2 = _gelu_tanh(project(u_base + p * MIX_SLAB))
            for gg in range(groups_per_slab):
                g = p * groups_per_slab + gg
                w = wtril_ref[g]
                bias = bs_ref[g]
                cols = slice(g * GROUP_DIM, (g + 1) * GROUP_DIM)
                for c in range(n_chunks):
                    rows = slice(c * CHUNK, (c + 1) * CHUNK)
                    mixed = jnp.dot(w, vsn_ref[rows, cols], preferred_element_type=F32) + bias
                    u = u2[rows, gg * GROUP_DIM:(gg + 1) * GROUP_DIM]
                    o_ref[rows, D_RET + g * GROUP_DIM: D_RET + (g + 1) * GROUP_DIM] = (u * mixed).astype(BF16)

        xn_ref[...] = xn_next_ref[...]


def _mix(x2d, positions, pre_g, w_in, ret_norm_g, sgu_norm_g, w_s, b_s):
    t = x2d.shape[0]
    seq = positions.shape[1]
    n_tiles = t // MIX_TM
    half = HEAD_DIM // 2
    inv_freq = ROPE_BASE ** (-jnp.arange(half, dtype=F32) / half)
    inv_freq = jnp.concatenate([inv_freq, inv_freq]).reshape(1, HEAD_DIM)
    n_slabs = D_PROJ // MIX_SLAB
    w_slabs = jnp.stack([w_in[:, k * MIX_SLAB:(k + 1) * MIX_SLAB] for k in range(n_slabs)]).astype(BF16)

    def const(shape):
        return pl.BlockSpec(shape, lambda s: (0,) * len(shape))

    return pl.pallas_call(
        functools.partial(_mix_kernel, seq // MIX_TM),
        grid=(n_tiles + 1,),
        in_specs=[
            pl.BlockSpec((MIX_TM, D_MODEL), lambda s: (jnp.minimum(s, n_tiles - 1), 0)),
            pl.BlockSpec((MIX_TM, 1), lambda s: (jnp.maximum(s - 1, 0), 0)),
            const((1, D_MODEL)),
            pl.BlockSpec((D_PROJ // MIX_SLAB, D_MODEL, MIX_SLAB), lambda s: (0, 0, 0),
                         pipeline_mode=pl.Buffered(1)),
            const((1, HEAD_DIM)),
            const((1, D_RET)),
            const((1, D_SGU)),
            const((SGU_GROUPS, CHUNK, CHUNK)),
            const((SGU_GROUPS, CHUNK, 1)),
        ],
        out_specs=pl.BlockSpec((MIX_TM, D_MODEL), lambda s: (jnp.maximum(s - 1, 0), 0)),
        out_shape=jax.ShapeDtypeStruct((t, D_MODEL), BF16),
        scratch_shapes=[
            pltpu.VMEM((MIX_TM, D_MODEL), BF16),
            pltpu.VMEM((MIX_TM, D_MODEL), BF16),
            pltpu.VMEM((MIX_TM, LANES), F32),
            pltpu.VMEM((RET_HEADS, HEAD_DIM, HEAD_DIM), F32),
            pltpu.VMEM((MIX_TM, D_SGU), F32),
            pltpu.VMEM((MIX_TM, D_SGU), BF16),
            pltpu.VMEM((3, RET_HEADS, CHUNK, CHUNK), F32),
            pltpu.VMEM((SGU_GROUPS, CHUNK, CHUNK), BF16),
        ],
        compiler_params=pltpu.CompilerParams(
            dimension_semantics=("arbitrary",),
            vmem_limit_bytes=VMEM_LIMIT_BYTES,
        ),
        name="token_mixing",
    )(x2d, positions.reshape(t, 1), pre_g, w_slabs, inv_freq, ret_norm_g, sgu_norm_g, w_s,
      b_s.reshape(SGU_GROUPS, CHUNK, 1))


def _out_proj_kernel(m_ref, w_ref, g_ref, x_ref, o_ref, y_ref, scale_ref):
    y_ref[...] = jnp.dot(m_ref[...], w_ref[...], preferred_element_type=F32)
    _store_row_scales(y_ref, scale_ref)
    for rb in range(OUT_TM // ROW_BLOCK):
        rows = slice(rb * ROW_BLOCK, (rb + 1) * ROW_BLOCK)
        scale = scale_ref[rows, :]
        for cb in range(D_MODEL // LANES):
            cols = slice(cb * LANES, (cb + 1) * LANES)
            o_ref[rows, cols] = x_ref[rows, cols] + (y_ref[rows, cols] * scale) * g_ref[:, cols]


def _out_proj(mix2d, w_out, g, x2d):
    t = x2d.shape[0]
    return pl.pallas_call(
        _out_proj_kernel,
        grid=(t // OUT_TM,),
        in_specs=[
            pl.BlockSpec((OUT_TM, D_MODEL), lambda i: (i, 0)),
            pl.BlockSpec((D_MODEL, D_MODEL), lambda i: (0, 0)),
            pl.BlockSpec((1, D_MODEL), lambda i: (0, 0)),
            pl.BlockSpec((OUT_TM, D_MODEL), lambda i: (i, 0)),
        ],
        out_specs=pl.BlockSpec((OUT_TM, D_MODEL), lambda i: (i, 0)),
        out_shape=jax.ShapeDtypeStruct((t, D_MODEL), F32),
        scratch_shapes=[pltpu.VMEM((OUT_TM, D_MODEL), F32), pltpu.VMEM((OUT_TM, LANES), F32)],
        compiler_params=pltpu.CompilerParams(
            dimension_semantics=("parallel",),
            vmem_limit_bytes=VMEM_LIMIT_BYTES,
        ),
        name="mix_out_proj",
    )(mix2d, w_out, g, x2d)


def kernel(x, positions, ffn1_pre_g, ffn1_w_gate, ffn1_w_up, ffn1_w_down, ffn1_post_g, mix_pre_g, w_in, ret_norm_g, sgu_norm_g, sgu_w_s, sgu_b_s, w_out, mix_post_g, ffn2_pre_g, ffn2_w_gate, ffn2_w_up, ffn2_w_down, ffn2_post_g):
    b, s, d = x.shape
    depth = ffn1_pre_g.shape[0]
    h = x.reshape(b * s, d)
    for l in range(depth):
        h = _ffn(h, ffn1_pre_g[l][None], ffn1_w_gate[l], ffn1_w_up[l], ffn1_w_down[l], ffn1_post_g[l][None])
        mixed = _mix(h, positions, mix_pre_g[l][None], w_in[l], ret_norm_g[l][None],
                     sgu_norm_g[l][None], sgu_w_s[l], sgu_b_s[l])
        h = _out_proj(mixed, w_out[l].astype(BF16), mix_post_g[l][None], h)
        h = _ffn(h, ffn2_pre_g[l][None], ffn2_w_gate[l], ffn2_w_up[l], ffn2_w_down[l], ffn2_post_g[l][None])
    return h.reshape(b, s, d)
```

```python
import functools
import math

import jax
import jax.numpy as jnp
from jax import lax
from jax.experimental import pallas as pl
from jax.experimental.pallas import tpu as pltpu

D_MODEL = 2048
D_RET = D_MODEL // 2
D_SGU = D_MODEL - D_RET
RET_HEADS = 8
HEAD_DIM = D_RET // RET_HEADS
SGU_GROUPS = 8
GROUP_DIM = D_SGU // SGU_GROUPS
CHUNK = 128
D_FF = 5632
ROPE_BASE = 10000.0
EPS = 1e-6
D_PROJ = 4 * D_RET + 2 * D_SGU

F32 = jnp.float32
BF16 = jnp.bfloat16

VMEM_LIMIT_BYTES = 56 * 1024 * 1024

FFN_TM = 1024
FFN_TF = 512
FFN_OUT_ROWS = 256
MIX_TM = 256
MIX_SLAB = 512
OUT_TM = 512
LANES = 128
ROW_BLOCK = 128


def _rms_scale(v):
    return lax.rsqrt(jnp.mean(v * v, axis=-1, keepdims=True) + EPS)


def _store_row_scales(v_ref, scale_ref):
    scale_ref[...] = jnp.broadcast_to(_rms_scale(v_ref[...]), scale_ref.shape)


def _rmsnorm_rows_to_bf16(x_ref, g_ref, xn_ref, scale_ref):
    _store_row_scales(x_ref, scale_ref)
    for rb in range(x_ref.shape[0] // ROW_BLOCK):
        rows = slice(rb * ROW_BLOCK, (rb + 1) * ROW_BLOCK)
        scale = scale_ref[rows, :]
        for cb in range(x_ref.shape[1] // LANES):
            cols = slice(cb * LANES, (cb + 1) * LANES)
            xn_ref[rows, cols] = ((x_ref[rows, cols] * scale) * g_ref[:, cols]).astype(BF16)


def _ffn_kernel(x_ref, pre_g_ref, wg_ref, wu_ref, wd_ref, post_g_ref, o_hbm, xn_ref, acc_ref,
                scale_ref, stage_ref, out_sem):
    i = pl.program_id(0)
    j = pl.program_id(1)

    @pl.when(j == 0)
    def _():
        _rmsnorm_rows_to_bf16(x_ref, pre_g_ref, xn_ref, scale_ref)

    @pl.when((i == 0) & (j == 0))
    def _():
        acc_ref[...] = jnp.zeros_like(acc_ref)

    xn = xn_ref[...]
    gate = jnp.dot(xn, wg_ref[...], preferred_element_type=F32)
    up = jnp.dot(xn, wu_ref[...], preferred_element_type=F32)
    half_gate = 0.5 * gate
    gu = half_gate * up
    h = (gu + gu * jnp.tanh(half_gate)).astype(BF16)
    carried = jnp.where(j == 0, 0.0, acc_ref[...])
    acc_ref[...] = carried + jnp.dot(h, wd_ref[...], preferred_element_type=F32)

    @pl.when(j == pl.num_programs(1) - 1)
    def _():
        n_copies = FFN_TM // FFN_OUT_ROWS

        def out_copy(tile, c):
            rows = pl.ds(tile * FFN_TM + c * FFN_OUT_ROWS, FFN_OUT_ROWS)
            return pltpu.make_async_copy(stage_ref.at[c], o_hbm.at[rows, :], out_sem.at[c])

        @pl.when(i > 0)
        def _():
            for c in range(n_copies):
                out_copy(i - 1, c).wait()

        _store_row_scales(acc_ref, scale_ref)
        half_g = 0.5 * post_g_ref[...]
        for c in range(n_copies):
            for rb in range(FFN_OUT_ROWS // ROW_BLOCK):
                dst = slice(rb * ROW_BLOCK, (rb + 1) * ROW_BLOCK)
                rows = slice(c * FFN_OUT_ROWS + rb * ROW_BLOCK, c * FFN_OUT_ROWS + (rb + 1) * ROW_BLOCK)
                scale = scale_ref[rows, :]
                for cb in range(D_MODEL // LANES):
                    cols = slice(cb * LANES, (cb + 1) * LANES)
                    normed = (acc_ref[rows, cols] * scale) * half_g[:, cols]
                    stage_ref[c, dst, cols] = x_ref[rows, cols] + normed
            out_copy(i, c).start()

        @pl.when(i == pl.num_programs(0) - 1)
        def _():
            for c in range(n_copies):
                out_copy(i, c).wait()


def _ffn(x2d, pre_g, w_gate, w_up, w_down, post_g):
    t = x2d.shape[0]
    grid = (t // FFN_TM, D_FF // FFN_TF)
    return pl.pallas_call(
        _ffn_kernel,
        grid=grid,
        in_specs=[
            pl.BlockSpec((FFN_TM, D_MODEL), lambda i, j: (i, 0)),
            pl.BlockSpec((1, D_MODEL), lambda i, j: (0, 0)),
            pl.BlockSpec((D_MODEL, FFN_TF), lambda i, j: (0, j)),
            pl.BlockSpec((D_MODEL, FFN_TF), lambda i, j: (0, j)),
            pl.BlockSpec((FFN_TF, D_MODEL), lambda i, j: (j, 0)),
            pl.BlockSpec((1, D_MODEL), lambda i, j: (0, 0)),
        ],
        out_specs=pl.BlockSpec(memory_space=pl.ANY),
        out_shape=jax.ShapeDtypeStruct((t, D_MODEL), F32),
        scratch_shapes=[
            pltpu.VMEM((FFN_TM, D_MODEL), BF16),
            pltpu.VMEM((FFN_TM, D_MODEL), F32),
            pltpu.VMEM((FFN_TM, LANES), F32),
            pltpu.VMEM((FFN_TM // FFN_OUT_ROWS, FFN_OUT_ROWS, D_MODEL), F32),
            pltpu.SemaphoreType.DMA((FFN_TM // FFN_OUT_ROWS,)),
        ],
        compiler_params=pltpu.CompilerParams(
            dimension_semantics=("arbitrary", "arbitrary"),
            vmem_limit_bytes=VMEM_LIMIT_BYTES,
        ),
        name="ffn_half_step",
    )(x2d, pre_g, w_gate, w_up, w_down, post_g)


def _log_gamma(h):
    return math.log1p(-(2.0 ** (-5.0 - h)))


def _layernorm_lanes(v):
    mu = jnp.mean(v, axis=-1, keepdims=True)
    vc = v - mu
    return vc * lax.rsqrt(jnp.mean(vc * vc, axis=-1, keepdims=True) + EPS)


def _gelu_tanh(v):
    c = math.sqrt(2.0 / math.pi)
    neg2y = v * ((v * v) * (-2.0 * c * 0.044715) + (-2.0 * c))
    return v / (1.0 + jnp.exp(neg2y))


def _mix_kernel(tiles_per_row, x_ref, pos_ref, pre_g_ref, w_ref, invf_ref, rg_ref, sg_ref, ws_ref,
                bs_ref, o_ref, xn_ref, xn_next_ref, scale_ref, state_ref, vs_ref, vsn_ref):
    s = pl.program_id(0)

    @pl.when(s == 0)
    def _():
        _rmsnorm_rows_to_bf16(x_ref, pre_g_ref, xn_ref, scale_ref)

    @pl.when((s > 0) & ((s - 1) % tiles_per_row == 0))
    def _():
        state_ref[...] = jnp.zeros_like(state_ref)

    @pl.when(s > 0)
    def _():
        xn = xn_ref[...]

        def project(first_col):
            return jnp.dot(xn, w_ref[first_col // MIX_SLAB], preferred_element_type=F32)

        u_base = 4 * D_RET
        for p in range(D_SGU // MIX_SLAB):
            vs_ref[:, p * MIX_SLAB:(p + 1) * MIX_SLAB] = _gelu_tanh(project(u_base + D_SGU + p * MIX_SLAB))

        _rmsnorm_rows_to_bf16(x_ref, pre_g_ref, xn_next_ref, scale_ref)
        vsn_ref[...] = (_layernorm_lanes(vs_ref[...]) * sg_ref[...]).astype(BF16)

        lane = lax.broadcasted_iota(jnp.int32, (1, HEAD_DIM), 1)
        low = lane < HEAD_DIM // 2
        pos = pos_ref[...].astype(F32)
        ang = jnp.where(low, pos[:MIX_TM // 2], pos[MIX_TM // 2:]) * invf_ref[...]
        cos_p, sin_p = jnp.cos(ang), jnp.sin(ang)
        cos_r, sin_r = pltpu.roll(cos_p, HEAD_DIM // 2, 1), pltpu.roll(sin_p, HEAD_DIM // 2, 1)
        cos_t = jnp.concatenate([jnp.where(low, cos_p, cos_r), jnp.where(low, cos_r, cos_p)], axis=0)
        sin_t = jnp.concatenate([jnp.where(low, -sin_p, sin_r), jnp.where(low, -sin_r, sin_p)], axis=0)

        n_chunks = MIX_TM // CHUNK
        row = lax.broadcasted_iota(jnp.int32, (CHUNK, CHUNK), 0)
        col = lax.broadcasted_iota(jnp.int32, (CHUNK, CHUNK), 1)
        causal = row >= col
        rel = jnp.where(causal, row - col, 0).astype(F32)
        row_f = row.astype(F32)
        q_scale = HEAD_DIM ** -0.5
        heads_per_slab = MIX_SLAB // HEAD_DIM

        for p in range(RET_HEADS // heads_per_slab):
            q2 = project(p * MIX_SLAB)
            k2 = project(D_RET + p * MIX_SLAB)
            v2 = project(2 * D_RET + p * MIX_SLAB).astype(BF16)
            g2 = project(3 * D_RET + p * MIX_SLAB)
            for hh in range(heads_per_slab):
                h = p * heads_per_slab + hh
                lanes = slice(hh * HEAD_DIM, (hh + 1) * HEAD_DIM)
                lg = _log_gamma(h)
                decay_in = jnp.where(causal, jnp.exp(lg * rel), 0.0) * q_scale
                xi = jnp.exp(lg * (row_f + 1.0)) * q_scale
                zeta = jnp.exp(lg * (CHUNK - 1.0 - row_f))
                chunk_decay = math.exp(lg * CHUNK)
                gain = rg_ref[:, h * HEAD_DIM:(h + 1) * HEAD_DIM]

                q = q2[:, lanes]
                k = k2[:, lanes]
                v = v2[:, lanes]
                gate = g2[:, lanes]
                q = q * cos_t + pltpu.roll(q, HEAD_DIM // 2, 1) * sin_t
                k = k * cos_t + pltpu.roll(k, HEAD_DIM // 2, 1) * sin_t

                state = state_ref[h]
                for c in range(n_chunks):
                    rows = slice(c * CHUNK, (c + 1) * CHUNK)
                    qc, kc, vc = q[rows], k[rows], v[rows]
                    scores = lax.dot_general(qc.astype(BF16), kc.astype(BF16), (((1,), (1,)), ((), ())),
                                             preferred_element_type=F32) * decay_in
                    inner = jnp.dot(scores.astype(BF16), vc, preferred_element_type=F32)
                    cross = jnp.dot((qc * xi).astype(BF16), state.astype(BF16), preferred_element_type=F32)
                    kv = lax.dot_general((kc * zeta).astype(BF16), vc, (((0,), (0,)), ((), ())),
                                         preferred_element_type=F32)
                    state = state * chunk_decay + kv
                    ret = _layernorm_lanes(inner + cross) * gain
                    gc = gate[rows]
                    o_ref[rows, h * HEAD_DIM:(h + 1) * HEAD_DIM] = (ret * (gc * jax.nn.sigmoid(gc))).astype(BF16)
                state_ref[h] = state

        groups_per_slab = MIX_SLAB // GROUP_DIM
        for p in range(SGU_GROUPS // groups_per_slab):
            u2 = _gelu_tanh(project(u_base + p * MIX_SLAB))
            for gg in range(groups_per_slab):
                g = p * groups_per_slab + gg
                w = jnp.where(causal, ws_ref[g], 0.0).astype(BF16)
                bias = bs_ref[g]
                cols = slice(g * GROUP_DIM, (g + 1) * GROUP_DIM)
                for c in range(n_chunks):
                    rows = slice(c * CHUNK, (c + 1) * CHUNK)
                    mixed = jnp.dot(w, vsn_ref[rows, cols], preferred_element_type=F32) + bias
                    u = u2[rows, gg * GROUP_DIM:(gg + 1) * GROUP_DIM]
                    o_ref[rows, D_RET + g * GROUP_DIM: D_RET + (g + 1) * GROUP_DIM] = (u * mixed).astype(BF16)

        xn_ref[...] = xn_next_ref[...]


def _mix(x2d, positions, pre_g, w_in, ret_norm_g, sgu_norm_g, w_s, b_s):
    t = x2d.shape[0]
    seq = positions.shape[1]
    n_tiles = t // MIX_TM
    half = HEAD_DIM // 2
    inv_freq = ROPE_BASE ** (-jnp.arange(half, dtype=F32) / half)
    inv_freq = jnp.concatenate([inv_freq, inv_freq]).reshape(1, HEAD_DIM)
    n_slabs = D_PROJ // MIX_SLAB
    w_slabs = jnp.stack([w_in[:, k * MIX_SLAB:(k + 1) * MIX_SLAB] for k in range(n_slabs)]).astype(BF16)

    def const(shape):
        return pl.BlockSpec(shape, lambda s: (0,) * len(shape))

    return pl.pallas_call(
        functools.partial(_mix_kernel, seq // MIX_TM),
        grid=(n_tiles + 1,),
        in_specs=[
            pl.BlockSpec((MIX_TM, D_MODEL), lambda s: (jnp.minimum(s, n_tiles - 1), 0)),
            pl.BlockSpec((MIX_TM, 1), lambda s: (jnp.maximum(s - 1, 0), 0)),
            const((1, D_MODEL)),
            pl.BlockSpec((D_PROJ // MIX_SLAB, D_MODEL, MIX_SLAB), lambda s: (0, 0, 0),
                         pipeline_mode=pl.Buffered(1)),
            const((1, HEAD_DIM)),
            const((1, D_RET)),
            const((1, D_SGU)),
            const((SGU_GROUPS, CHUNK, CHUNK)),
            const((SGU_GROUPS, CHUNK, 1)),
        ],
        out_specs=pl.BlockSpec((MIX_TM, D_MODEL), lambda s: (jnp.maximum(s - 1, 0), 0)),
        out_shape=jax.ShapeDtypeStruct((t, D_MODEL), BF16),
        scratch_shapes=[
            pltpu.VMEM((MIX_TM, D_MODEL), BF16),
            pltpu.VMEM((MIX_TM, D_MODEL), BF16),
            pltpu.VMEM((MIX_TM, LANES), F32),
            pltpu.VMEM((RET_HEADS, HEAD_DIM, HEAD_DIM), F32),
            pltpu.VMEM((MIX_TM, D_SGU), F32),
            pltpu.VMEM((MIX_TM, D_SGU), BF16),
        ],
        compiler_params=pltpu.CompilerParams(
            dimension_semantics=("arbitrary",),
            vmem_limit_bytes=VMEM_LIMIT_BYTES,
        ),
        name="token_mixing",
    )(x2d, positions.reshape(t, 1), pre_g, w_slabs, inv_freq, ret_norm_g, sgu_norm_g, w_s,
      b_s.reshape(SGU_GROUPS, CHUNK, 1))


def _out_proj_kernel(m_ref, w_ref, g_ref, x_ref, o_ref, y_ref, scale_ref):
    y_ref[...] = jnp.dot(m_ref[...], w_ref[...], preferred_element_type=F32)
    _store_row_scales(y_ref, scale_ref)
    for rb in range(OUT_TM // ROW_BLOCK):
        rows = slice(rb * ROW_BLOCK, (rb + 1) * ROW_BLOCK)
        scale = scale_ref[rows, :]
        for cb in range(D_MODEL // LANES):
            cols = slice(cb * LANES, (cb + 1) * LANES)
            o_ref[rows, cols] = x_ref[rows, cols] + (y_ref[rows, cols] * scale) * g_ref[:, cols]


def _out_proj(mix2d, w_out, g, x2d):
    t = x2d.shape[0]
    return pl.pallas_call(
        _out_proj_kernel,
        grid=(t // OUT_TM,),
        in_specs=[
            pl.BlockSpec((OUT_TM, D_MODEL), lambda i: (i, 0)),
            pl.BlockSpec((D_MODEL, D_MODEL), lambda i: (0, 0)),
            pl.BlockSpec((1, D_MODEL), lambda i: (0, 0)),
            pl.BlockSpec((OUT_TM, D_MODEL), lambda i: (i, 0)),
        ],
        out_specs=pl.BlockSpec((OUT_TM, D_MODEL), lambda i: (i, 0)),
        out_shape=jax.ShapeDtypeStruct((t, D_MODEL), F32),
        scratch_shapes=[pltpu.VMEM((OUT_TM, D_MODEL), F32), pltpu.VMEM((OUT_TM, LANES), F32)],
        compiler_params=pltpu.CompilerParams(
            dimension_semantics=("parallel",),
            vmem_limit_bytes=VMEM_LIMIT_BYTES,
        ),
        name="mix_out_proj",
    )(mix2d, w_out, g, x2d)


def kernel(x, positions, ffn1_pre_g, ffn1_w_gate, ffn1_w_up, ffn1_w_down, ffn1_post_g, mix_pre_g, w_in, ret_norm_g, sgu_norm_g, sgu_w_s, sgu_b_s, w_out, mix_post_g, ffn2_pre_g, ffn2_w_gate, ffn2_w_up, ffn2_w_down, ffn2_post_g):
    b, s, d = x.shape
    depth = ffn1_pre_g.shape[0]
    h = x.reshape(b * s, d)
    for l in range(depth):
        h = _ffn(h, ffn1_pre_g[l][None], ffn1_w_gate[l].astype(BF16), ffn1_w_up[l].astype(BF16),
                 ffn1_w_down[l].astype(BF16), ffn1_post_g[l][None])
        mixed = _mix(h, positions, mix_pre_g[l][None], w_in[l], ret_norm_g[l][None],
                     sgu_norm_g[l][None], sgu_w_s[l], sgu_b_s[l])
        h = _out_proj(mixed, w_out[l].astype(BF16), mix_post_g[l][None], h)
        h = _ffn(h, ffn2_pre_g[l][None], ffn2_w_gate[l].astype(BF16), ffn2_w_up[l].astype(BF16),
                 ffn2_w_down[l].astype(BF16), ffn2_post_g[l][None])
    return h.reshape(b, s, d)
```

```python
import functools
import math

import jax
import jax.numpy as jnp
from jax import lax
from jax.experimental import pallas as pl
from jax.experimental.pallas import tpu as pltpu

D_MODEL = 2048
D_RET = D_MODEL // 2
D_SGU = D_MODEL - D_RET
RET_HEADS = 8
HEAD_DIM = D_RET // RET_HEADS
SGU_GROUPS = 8
GROUP_DIM = D_SGU // SGU_GROUPS
CHUNK = 128
D_FF = 5632
ROPE_BASE = 10000.0
EPS = 1e-6
D_PROJ = 4 * D_RET + 2 * D_SGU

F32 = jnp.float32
BF16 = jnp.bfloat16

VMEM_LIMIT_BYTES = 56 * 1024 * 1024

FFN_TM = 1024
FFN_TF = 512
FFN_OUT_ROWS = 256
MIX_TM = 256
MIX_SLAB = 512
OUT_TM = 512
LANES = 128
ROW_BLOCK = 64


def _rms_scale(v):
    return lax.rsqrt(jnp.mean(v * v, axis=-1, keepdims=True) + EPS)


def _store_row_scales(v_ref, scale_ref):
    scale_ref[...] = jnp.broadcast_to(_rms_scale(v_ref[...]), scale_ref.shape)


def _rmsnorm_rows_to_bf16(x_ref, g_ref, xn_ref, scale_ref):
    _store_row_scales(x_ref, scale_ref)
    for rb in range(x_ref.shape[0] // ROW_BLOCK):
        rows = slice(rb * ROW_BLOCK, (rb + 1) * ROW_BLOCK)
        scale = scale_ref[rows, :]
        for cb in range(x_ref.shape[1] // LANES):
            cols = slice(cb * LANES, (cb + 1) * LANES)
            xn_ref[rows, cols] = ((x_ref[rows, cols] * scale) * g_ref[:, cols]).astype(BF16)


def _ffn_kernel(x_ref, pre_g_ref, wg_ref, wu_ref, wd_ref, post_g_ref, o_hbm, xn_ref, acc_ref,
                scale_ref, stage_ref, out_sem):
    i = pl.program_id(0)
    j = pl.program_id(1)

    @pl.when(j == 0)
    def _():
        _rmsnorm_rows_to_bf16(x_ref, pre_g_ref, xn_ref, scale_ref)

    @pl.when((i == 0) & (j == 0))
    def _():
        acc_ref[...] = jnp.zeros_like(acc_ref)

    xn = xn_ref[...]
    gate = jnp.dot(xn, wg_ref[...], preferred_element_type=F32)
    up = jnp.dot(xn, wu_ref[...], preferred_element_type=F32)
    half_gate = 0.5 * gate
    gu = half_gate * up
    h = (gu + gu * jnp.tanh(half_gate)).astype(BF16)
    carried = jnp.where(j == 0, 0.0, acc_ref[...])
    acc_ref[...] = carried + jnp.dot(h, wd_ref[...], preferred_element_type=F32)

    @pl.when(j == pl.num_programs(1) - 1)
    def _():
        n_copies = FFN_TM // FFN_OUT_ROWS

        def out_copy(tile, c):
            rows = pl.ds(tile * FFN_TM + c * FFN_OUT_ROWS, FFN_OUT_ROWS)
            return pltpu.make_async_copy(stage_ref.at[c], o_hbm.at[rows, :], out_sem.at[c])

        @pl.when(i > 0)
        def _():
            for c in range(n_copies):
                out_copy(i - 1, c).wait()

        _store_row_scales(acc_ref, scale_ref)
        half_g = 0.5 * post_g_ref[...]
        for c in range(n_copies):
            for rb in range(FFN_OUT_ROWS // ROW_BLOCK):
                dst = slice(rb * ROW_BLOCK, (rb + 1) * ROW_BLOCK)
                rows = slice(c * FFN_OUT_ROWS + rb * ROW_BLOCK, c * FFN_OUT_ROWS + (rb + 1) * ROW_BLOCK)
                scale = scale_ref[rows, :]
                for cb in range(D_MODEL // LANES):
                    cols = slice(cb * LANES, (cb + 1) * LANES)
                    normed = (acc_ref[rows, cols] * scale) * half_g[:, cols]
                    stage_ref[c, dst, cols] = x_ref[rows, cols] + normed
            out_copy(i, c).start()

        @pl.when(i == pl.num_programs(0) - 1)
        def _():
            for c in range(n_copies):
                out_copy(i, c).wait()


def _ffn(x2d, pre_g, w_gate, w_up, w_down, post_g):
    t = x2d.shape[0]
    grid = (t // FFN_TM, D_FF // FFN_TF)
    return pl.pallas_call(
        _ffn_kernel,
        grid=grid,
        in_specs=[
            pl.BlockSpec((FFN_TM, D_MODEL), lambda i, j: (i, 0)),
            pl.BlockSpec((1, D_MODEL), lambda i, j: (0, 0)),
            pl.BlockSpec((D_MODEL, FFN_TF), lambda i, j: (0, j)),
            pl.BlockSpec((D_MODEL, FFN_TF), lambda i, j: (0, j)),
            pl.BlockSpec((FFN_TF, D_MODEL), lambda i, j: (j, 0)),
            pl.BlockSpec((1, D_MODEL), lambda i, j: (0, 0)),
        ],
        out_specs=pl.BlockSpec(memory_space=pl.ANY),
        out_shape=jax.ShapeDtypeStruct((t, D_MODEL), F32),
        scratch_shapes=[
            pltpu.VMEM((FFN_TM, D_MODEL), BF16),
            pltpu.VMEM((FFN_TM, D_MODEL), F32),
            pltpu.VMEM((FFN_TM, LANES), F32),
            pltpu.VMEM((FFN_TM // FFN_OUT_ROWS, FFN_OUT_ROWS, D_MODEL), F32),
            pltpu.SemaphoreType.DMA((FFN_TM // FFN_OUT_ROWS,)),
        ],
        compiler_params=pltpu.CompilerParams(
            dimension_semantics=("arbitrary", "arbitrary"),
            vmem_limit_bytes=VMEM_LIMIT_BYTES,
        ),
        name="ffn_half_step",
    )(x2d, pre_g, w_gate, w_up, w_down, post_g)


def _log_gamma(h):
    return math.log1p(-(2.0 ** (-5.0 - h)))


def _layernorm_lanes(v):
    mu = jnp.mean(v, axis=-1, keepdims=True)
    vc = v - mu
    return vc * lax.rsqrt(jnp.mean(vc * vc, axis=-1, keepdims=True) + EPS)


def _gelu_tanh(v):
    c = math.sqrt(2.0 / math.pi)
    neg2y = v * ((v * v) * (-2.0 * c * 0.044715) + (-2.0 * c))
    return v / (1.0 + jnp.exp(neg2y))


def _mix_kernel(tiles_per_row, x_ref, pos_ref, pre_g_ref, w_ref, invf_ref, rg_ref, sg_ref, ws_ref,
                bs_ref, o_ref, xn_ref, xn_next_ref, scale_ref, state_ref, vs_ref, vsn_ref):
    s = pl.program_id(0)

    @pl.when(s == 0)
    def _():
        _rmsnorm_rows_to_bf16(x_ref, pre_g_ref, xn_ref, scale_ref)

    @pl.when((s > 0) & ((s - 1) % tiles_per_row == 0))
    def _():
        state_ref[...] = jnp.zeros_like(state_ref)

    @pl.when(s > 0)
    def _():
        xn = xn_ref[...]

        def project(first_col):
            return jnp.dot(xn, w_ref[first_col // MIX_SLAB], preferred_element_type=F32)

        u_base = 4 * D_RET
        for p in range(D_SGU // MIX_SLAB):
            vs_ref[:, p * MIX_SLAB:(p + 1) * MIX_SLAB] = _gelu_tanh(project(u_base + D_SGU + p * MIX_SLAB))

        _rmsnorm_rows_to_bf16(x_ref, pre_g_ref, xn_next_ref, scale_ref)
        vsn_ref[...] = (_layernorm_lanes(vs_ref[...]) * sg_ref[...]).astype(BF16)

        lane = lax.broadcasted_iota(jnp.int32, (1, HEAD_DIM), 1)
        low = lane < HEAD_DIM // 2
        pos = pos_ref[...].astype(F32)
        ang = jnp.where(low, pos[:MIX_TM // 2], pos[MIX_TM // 2:]) * invf_ref[...]
        cos_p, sin_p = jnp.cos(ang), jnp.sin(ang)
        cos_r, sin_r = pltpu.roll(cos_p, HEAD_DIM // 2, 1), pltpu.roll(sin_p, HEAD_DIM // 2, 1)
        cos_t = jnp.concatenate([jnp.where(low, cos_p, cos_r), jnp.where(low, cos_r, cos_p)], axis=0)
        sin_t = jnp.concatenate([jnp.where(low, -sin_p, sin_r), jnp.where(low, -sin_r, sin_p)], axis=0)

        n_chunks = MIX_TM // CHUNK
        row = lax.broadcasted_iota(jnp.int32, (CHUNK, CHUNK), 0)
        col = lax.broadcasted_iota(jnp.int32, (CHUNK, CHUNK), 1)
        causal = row >= col
        rel = jnp.where(causal, row - col, 0).astype(F32)
        row_f = row.astype(F32)
        q_scale = HEAD_DIM ** -0.5
        heads_per_slab = MIX_SLAB // HEAD_DIM

        for p in range(RET_HEADS // heads_per_slab):
            q2 = project(p * MIX_SLAB)
            k2 = project(D_RET + p * MIX_SLAB)
            v2 = project(2 * D_RET + p * MIX_SLAB).astype(BF16)
            g2 = project(3 * D_RET + p * MIX_SLAB)
            for hh in range(heads_per_slab):
                h = p * heads_per_slab + hh
                lanes = slice(hh * HEAD_DIM, (hh + 1) * HEAD_DIM)
                lg = _log_gamma(h)
                decay_in = jnp.where(causal, jnp.exp(lg * rel), 0.0) * q_scale
                xi = jnp.exp(lg * (row_f + 1.0)) * q_scale
                zeta = jnp.exp(lg * (CHUNK - 1.0 - row_f))
                chunk_decay = math.exp(lg * CHUNK)
                gain = rg_ref[:, h * HEAD_DIM:(h + 1) * HEAD_DIM]

                q = q2[:, lanes]
                k = k2[:, lanes]
                v = v2[:, lanes]
                gate = g2[:, lanes]
                q = q * cos_t + pltpu.roll(q, HEAD_DIM // 2, 1) * sin_t
                k = k * cos_t + pltpu.roll(k, HEAD_DIM // 2, 1) * sin_t

                state = state_ref[h]
                for c in range(n_chunks):
                    rows = slice(c * CHUNK, (c + 1) * CHUNK)
                    qc, kc, vc = q[rows], k[rows], v[rows]
                    scores = lax.dot_general(qc.astype(BF16), kc.astype(BF16), (((1,), (1,)), ((), ())),
                                             preferred_element_type=F32) * decay_in
                    inner = jnp.dot(scores.astype(BF16), vc, preferred_element_type=F32)
                    cross = jnp.dot((qc * xi).astype(BF16), state.astype(BF16), preferred_element_type=F32)
                    kv = lax.dot_general((kc * zeta).astype(BF16), vc, (((0,), (0,)), ((), ())),
                                         preferred_element_type=F32)
                    state = state * chunk_decay + kv
                    ret = _layernorm_lanes(inner + cross) * gain
                    gc = gate[rows]
                    o_ref[rows, h * HEAD_DIM:(h + 1) * HEAD_DIM] = (ret * (gc * jax.nn.sigmoid(gc))).astype(BF16)
                state_ref[h] = state

        groups_per_slab = MIX_SLAB // GROUP_DIM
        for p in range(SGU_GROUPS // groups_per_slab):
            u2 = _gelu_tanh(project(u_base + p * MIX_SLAB))
            for gg in range(groups_per_slab):
                g = p * groups_per_slab + gg
                w = jnp.where(causal, ws_ref[g], 0.0).astype(BF16)
                bias = bs_ref[g]
                cols = slice(g * GROUP_DIM, (g + 1) * GROUP_DIM)
                for c in range(n_chunks):
                    rows = slice(c * CHUNK, (c + 1) * CHUNK)
                    mixed = jnp.dot(w, vsn_ref[rows, cols], preferred_element_type=F32) + bias
                    u = u2[rows, gg * GROUP_DIM:(gg + 1) * GROUP_DIM]
                    o_ref[rows, D_RET + g * GROUP_DIM: D_RET + (g + 1) * GROUP_DIM] = (u * mixed).astype(BF16)

        xn_ref[...] = xn_next_ref[...]


def _mix(x2d, positions, pre_g, w_in, ret_norm_g, sgu_norm_g, w_s, b_s):
    t = x2d.shape[0]
    seq = positions.shape[1]
    n_tiles = t // MIX_TM
    half = HEAD_DIM // 2
    inv_freq = ROPE_BASE ** (-jnp.arange(half, dtype=F32) / half)
    inv_freq = jnp.concatenate([inv_freq, inv_freq]).reshape(1, HEAD_DIM)
    n_slabs = D_PROJ // MIX_SLAB
    w_slabs = jnp.stack([w_in[:, k * MIX_SLAB:(k + 1) * MIX_SLAB] for k in range(n_slabs)]).astype(BF16)

    def const(shape):
        return pl.BlockSpec(shape, lambda s: (0,) * len(shape))

    return pl.pallas_call(
        functools.partial(_mix_kernel, seq // MIX_TM),
        grid=(n_tiles + 1,),
        in_specs=[
            pl.BlockSpec((MIX_TM, D_MODEL), lambda s: (jnp.minimum(s, n_tiles - 1), 0)),
            pl.BlockSpec((MIX_TM, 1), lambda s: (jnp.maximum(s - 1, 0), 0)),
            const((1, D_MODEL)),
            pl.BlockSpec((D_PROJ // MIX_SLAB, D_MODEL, MIX_SLAB), lambda s: (0, 0, 0),
                         pipeline_mode=pl.Buffered(1)),
            const((1, HEAD_DIM)),
            const((1, D_RET)),
            const((1, D_SGU)),
            const((SGU_GROUPS, CHUNK, CHUNK)),
            const((SGU_GROUPS, CHUNK, 1)),
        ],
        out_specs=pl.BlockSpec((MIX_TM, D_MODEL), lambda s: (jnp.maximum(s - 1, 0), 0)),
        out_shape=jax.ShapeDtypeStruct((t, D_MODEL), BF16),
        scratch_shapes=[
            pltpu.VMEM((MIX_TM, D_MODEL), BF16),
            pltpu.VMEM((MIX_TM, D_MODEL), BF16),
            pltpu.VMEM((MIX_TM, LANES), F32),
            pltpu.VMEM((RET_HEADS, HEAD_DIM, HEAD_DIM), F32),
            pltpu.VMEM((MIX_TM, D_SGU), F32),
            pltpu.VMEM((MIX_TM, D_SGU), BF16),
        ],
        compiler_params=pltpu.CompilerParams(
            dimension_semantics=("arbitrary",),
            vmem_limit_bytes=VMEM_LIMIT_BYTES,
        ),
        name="token_mixing",
    )(x2d, positions.reshape(t, 1), pre_g, w_slabs, inv_freq, ret_norm_g, sgu_norm_g, w_s,
      b_s.reshape(SGU_GROUPS, CHUNK, 1))


def _out_proj_kernel(m_ref, w_ref, g_ref, x_ref, o_ref, y_ref, scale_ref):
    y_ref[...] = jnp.dot(m_ref[...], w_ref[...], preferred_element_type=F32)
    _store_row_scales(y_ref, scale_ref)
    for rb in range(OUT_TM // ROW_BLOCK):
        rows = slice(rb * ROW_BLOCK, (rb + 1) * ROW_BLOCK)
        scale = scale_ref[rows, :]
        for cb in range(D_MODEL // LANES):
            cols = slice(cb * LANES, (cb + 1) * LANES)
            o_ref[rows, cols] = x_ref[rows, cols] + (y_ref[rows, cols] * scale) * g_ref[:, cols]


def _out_proj(mix2d, w_out, g, x2d):
    t = x2d.shape[0]
    return pl.pallas_call(
        _out_proj_kernel,
        grid=(t // OUT_TM,),
        in_specs=[
            pl.BlockSpec((OUT_TM, D_MODEL), lambda i: (i, 0)),
            pl.BlockSpec((D_MODEL, D_MODEL), lambda i: (0, 0)),
            pl.BlockSpec((1, D_MODEL), lambda i: (0, 0)),
            pl.BlockSpec((OUT_TM, D_MODEL), lambda i: (i, 0)),
        ],
        out_specs=pl.BlockSpec((OUT_TM, D_MODEL), lambda i: (i, 0)),
        out_shape=jax.ShapeDtypeStruct((t, D_MODEL), F32),
        scratch_shapes=[pltpu.VMEM((OUT_TM, D_MODEL), F32), pltpu.VMEM((OUT_TM, LANES), F32)],
        compiler_params=pltpu.CompilerParams(
            dimension_semantics=("parallel",),
            vmem_limit_bytes=VMEM_LIMIT_BYTES,
        ),
        name="mix_out_proj",
    )(mix2d, w_out, g, x2d)


def kernel(x, positions, ffn1_pre_g, ffn1_w_gate, ffn1_w_up, ffn1_w_down, ffn1_post_g, mix_pre_g, w_in, ret_norm_g, sgu_norm_g, sgu_w_s, sgu_b_s, w_out, mix_post_g, ffn2_pre_g, ffn2_w_gate, ffn2_w_up, ffn2_w_down, ffn2_post_g):
    b, s, d = x.shape
    depth = ffn1_pre_g.shape[0]
    h = x.reshape(b * s, d)
    for l in range(depth):
        h = _ffn(h, ffn1_pre_g[l][None], ffn1_w_gate[l].astype(BF16), ffn1_w_up[l].astype(BF16),
                 ffn1_w_down[l].astype(BF16), ffn1_post_g[l][None])
        mixed = _mix(h, positions, mix_pre_g[l][None], w_in[l], ret_norm_g[l][None],
                     sgu_norm_g[l][None], sgu_w_s[l], sgu_b_s[l])
        h = _out_proj(mixed, w_out[l].astype(BF16), mix_post_g[l][None], h)
        h = _ffn(h, ffn2_pre_g[l][None], ffn2_w_gate[l].astype(BF16), ffn2_w_up[l].astype(BF16),
                 ffn2_w_down[l].astype(BF16), ffn2_post_g[l][None])
    return h.reshape(b, s, d)
```
